```python
import jax, jax.numpy as jnp
from jax import lax
import numpy as np

D_MODEL = 1024
BATCH = 16
SEQ = 256
DEPTH = 4
DEC_BATCH = 4
DEC_SEQ = 4096
PAST_LEN = 256

GRID_W = 64
N_MIXERS = 3
N_A = (DEPTH + 2) // 3
N_B = (DEPTH + 1) // 3
N_C = DEPTH // 3
N_HEADS = 16
KV_HEADS = 4
GROUP = N_HEADS // KV_HEADS
HEAD_DIM = 64
QKV_DIM = (N_HEADS + 2 * KV_HEADS) * HEAD_DIM
WINDOW = 128
BLOCK = 128
Q_LORA = 384
KV_LORA = 256
NOPE_DIM = 64
ROPE_DIM = 32
V_DIM = 64
D_FF = 4 * D_MODEL
ROPE_THETA = 10000.0
EPS = 1e-6
NEG_INF = -1e30
ATTN_SCALE = HEAD_DIM ** -0.5
MLA_SCALE = (NOPE_DIM + ROPE_DIM) ** -0.5

kernel_name = 'hybrid_diffusion_prefix_trunk_step'


def rmsnorm(x, g):
    xf = x.astype(jnp.float32)
    y = xf * lax.rsqrt(jnp.mean(xf * xf, axis=-1, keepdims=True) + EPS)
    return (y * g.astype(jnp.float32)).astype(x.dtype)


def grid_positions(t_len):
    n_rows = t_len // GRID_W
    rows = jnp.repeat(jnp.arange(n_rows), GRID_W).astype(jnp.float32)
    cols = jnp.tile(jnp.arange(GRID_W), n_rows).astype(jnp.float32)
    return rows, cols


def _rope_1d(x, pos):
    d = x.shape[-1]
    freqs = ROPE_THETA ** (-jnp.arange(0, d, 2, dtype=jnp.float32) / d)
    ang = pos[:, None] * freqs[None, :]
    cos = jnp.cos(ang)[:, None, :].astype(x.dtype)
    sin = jnp.sin(ang)[:, None, :].astype(x.dtype)
    x1, x2 = jnp.split(x, 2, axis=-1)
    return jnp.concatenate([x1 * cos - x2 * sin, x1 * sin + x2 * cos], axis=-1)


def rope_2d(x, rows, cols):
    half = x.shape[-1] // 2
    return jnp.concatenate([_rope_1d(x[..., :half], rows), _rope_1d(x[..., half:], cols)], axis=-1)


def ada_mod(cond, w, b):
    m = jax.nn.silu(cond) @ w + b
    return jnp.split(m[..., None, :], 6, axis=-1)


def modulate(h, shift, scale):
    return h * (1.0 + scale) + shift


def attend(q, k, v, scale, sink=None, mask=None):
    s = jnp.einsum('bqhgd,bkhd->bhgqk', q, k, preferred_element_type=jnp.float32) * scale
    if mask is not None:
        s = jnp.where(mask, s, NEG_INF)
    if sink is not None:
        sink_col = jnp.broadcast_to(sink.astype(jnp.float32)[None, :, :, None, None], s.shape[:-1] + (1,))
        p = jax.nn.softmax(jnp.concatenate([sink_col, s], axis=-1), axis=-1)[..., 1:]
    else:
        p = jax.nn.softmax(s, axis=-1)
    return jnp.einsum('bhgqk,bkhd->bqhgd', p.astype(v.dtype), v)


def merge_blocks(o):
    o = jnp.moveaxis(o, 0, 1)
    return o.reshape((o.shape[0], o.shape[1] * o.shape[2]) + o.shape[3:])


def blocked_attend(q, k, v, scale, sink=None):
    t_len = q.shape[1]

    def one(b):
        qb = lax.dynamic_slice_in_dim(q, b * BLOCK, BLOCK, axis=1)
        return attend(qb, k, v, scale, sink)

    return merge_blocks(lax.map(one, jnp.arange(t_len // BLOCK)))


def split_qkv(qkv):
    b, t, _ = qkv.shape
    q, k, v = jnp.split(qkv, [N_HEADS * HEAD_DIM, (N_HEADS + KV_HEADS) * HEAD_DIM], axis=-1)
    return (q.reshape(b, t, N_HEADS, HEAD_DIM), k.reshape(b, t, KV_HEADS, HEAD_DIM),
            v.reshape(b, t, KV_HEADS, HEAD_DIM))


def mixer_a_context(h, w_qkv, sink, w_o):
    b, s, _ = h.shape
    q, k, v = split_qkv(h @ w_qkv)
    o = blocked_attend(q.reshape(b, s, KV_HEADS, GROUP, HEAD_DIM), k, v, ATTN_SCALE, sink.reshape(KV_HEADS, GROUP))
    return o.reshape(b, s, N_HEADS * HEAD_DIM) @ w_o, k, v


def mixer_a_latent(h, ctx_k, ctx_v, w_qkv, sink, w_o, rows, cols):
    b, t, _ = h.shape
    p_len = ctx_k.shape[1]
    q, k, v = split_qkv(h @ w_qkv)
    q = rope_2d(q, rows, cols).reshape(b, t, KV_HEADS, GROUP, HEAD_DIM)
    k = rope_2d(k, rows, cols)
    pad = ((0, 0), (BLOCK, BLOCK), (0, 0), (0, 0))
    kp = jnp.pad(k, pad)
    vp = jnp.pad(v, pad)
    q_off = jnp.arange(BLOCK)[:, None]
    k_off = jnp.arange(3 * BLOCK)[None, :] - BLOCK
    ctx_mask = jnp.ones((BLOCK, p_len), dtype=bool)
    sink_g = sink.reshape(KV_HEADS, GROUP)

    def one(blk):
        start = blk * BLOCK
        qb = lax.dynamic_slice_in_dim(q, start, BLOCK, axis=1)
        kb = lax.dynamic_slice_in_dim(kp, start, 3 * BLOCK, axis=1)
        vb = lax.dynamic_slice_in_dim(vp, start, 3 * BLOCK, axis=1)
        kpos = start + k_off
        win = (jnp.abs(q_off - k_off) <= WINDOW) & (kpos >= 0) & (kpos < t)
        mask = jnp.concatenate([ctx_mask, win], axis=1)
        return attend(qb, jnp.concatenate([ctx_k, kb], axis=1), jnp.concatenate([ctx_v, vb], axis=1),
                      ATTN_SCALE, sink_g, mask)

    o = merge_blocks(lax.map(one, jnp.arange(t // BLOCK)))
    return o.reshape(b, t, N_HEADS * HEAD_DIM) @ w_o


def mla_project(h, w_dq, g_q, w_uq, w_dkv, g_kv):
    b, t, _ = h.shape
    q = (rmsnorm(h @ w_dq, g_q) @ w_uq).reshape(b, t, N_HEADS, NOPE_DIM + ROPE_DIM)
    ckv = h @ w_dkv
    c_kv = rmsnorm(ckv[..., :KV_LORA], g_kv)
    k_rope = ckv[..., KV_LORA:]
    return q, c_kv, k_rope


def mla_expand(c_kv, k_rope, w_ukv):
    b, t, _ = c_kv.shape
    kv = (c_kv @ w_ukv).reshape(b, t, N_HEADS, NOPE_DIM + V_DIM)
    k_nope, v = kv[..., :NOPE_DIM], kv[..., NOPE_DIM:]
    k = jnp.concatenate([k_nope, jnp.broadcast_to(k_rope[:, :, None, :], (b, t, N_HEADS, ROPE_DIM))], axis=-1)
    return k, v


def mixer_b_context(h, w_dq, g_q, w_uq, w_dkv, g_kv, w_ukv, w_o):
    b, s, _ = h.shape
    q, c_kv, k_rope = mla_project(h, w_dq, g_q, w_uq, w_dkv, g_kv)
    k, v = mla_expand(c_kv, k_rope, w_ukv)
    o = blocked_attend(q[:, :, :, None, :], k, v, MLA_SCALE)
    return o.reshape(b, s, N_HEADS * V_DIM) @ w_o, c_kv, k_rope


def mixer_b_latent(h, ctx_ckv, ctx_krope, w_dq, g_q, w_uq, w_dkv, g_kv, w_ukv, w_o, rows, cols):
    b, t, _ = h.shape
    q, c_kv, k_rope = mla_project(h, w_dq, g_q, w_uq, w_dkv, g_kv)
    q = jnp.concatenate([q[..., :NOPE_DIM], rope_2d(q[..., NOPE_DIM:], rows, cols)], axis=-1)
    k_rope = rope_2d(k_rope[:, :, None, :], rows, cols)[:, :, 0, :]
    k_lat, v_lat = mla_expand(c_kv, k_rope, w_ukv)
    k_ctx, v_ctx = mla_expand(ctx_ckv, ctx_krope, w_ukv)
    k_all = jnp.concatenate([k_ctx, k_lat], axis=1)
    v_all = jnp.concatenate([v_ctx, v_lat], axis=1)
    o = blocked_attend(q[:, :, :, None, :], k_all, v_all, MLA_SCALE)
    return o.reshape(b, t, N_HEADS * V_DIM) @ w_o


def mixer_c_context(h, w_qkv, g_q, g_k, w_o):
    b, s, _ = h.shape
    q, k, v = split_qkv(h @ w_qkv)
    q = rmsnorm(q, g_q)
    k = rmsnorm(k, g_k)
    o = blocked_attend(q.reshape(b, s, KV_HEADS, GROUP, HEAD_DIM), k, v, ATTN_SCALE)
    return o.reshape(b, s, N_HEADS * HEAD_DIM) @ w_o, k, v


def mixer_c_latent(h, ctx_k, ctx_v, w_qkv, g_q, g_k, w_o, rows, cols):
    b, t, _ = h.shape
    q, k, v = split_qkv(h @ w_qkv)
    q = rope_2d(rmsnorm(q, g_q), rows, cols).reshape(b, t, KV_HEADS, GROUP, HEAD_DIM)
    k = rope_2d(rmsnorm(k, g_k), rows, cols)
    k_all = jnp.concatenate([ctx_k, k], axis=1)
    v_all = jnp.concatenate([ctx_v, v], axis=1)
    o = blocked_attend(q, k_all, v_all, ATTN_SCALE)
    return o.reshape(b, t, N_HEADS * HEAD_DIM) @ w_o


def sq_relu_mlp(h, w_in, w_out):
    return jnp.square(jax.nn.relu(h @ w_in)) @ w_out


def setup_inputs(seed: int = 0) -> dict:
    key = jax.random.key(seed)
    ks = jax.random.split(key, 32)

    def nrm(k, shape, scale=1.0):
        return jax.random.normal(k, shape, jnp.float32) * scale

    def gain(k, shape):
        return 1.0 + 0.02 * jax.random.normal(k, shape, jnp.float32)

    d = D_MODEL
    return {
        'x_prompt': nrm(ks[0], (BATCH, SEQ, d)),
        'x_sample': nrm(ks[1], (DEC_BATCH, DEC_SEQ, d)),
        'c': nrm(ks[2], (DEC_BATCH, d)),
        'cache_a_k': nrm(ks[3], (DEC_BATCH, N_A, PAST_LEN, KV_HEADS, HEAD_DIM)),
        'cache_a_v': nrm(ks[4], (DEC_BATCH, N_A, PAST_LEN, KV_HEADS, HEAD_DIM)),
        'cache_b_ckv': nrm(ks[5], (DEC_BATCH, N_B, PAST_LEN, KV_LORA)),
        'cache_b_krope': nrm(ks[6], (DEC_BATCH, N_B, PAST_LEN, ROPE_DIM)),
        'cache_c_k': nrm(ks[7], (DEC_BATCH, N_C, PAST_LEN, KV_HEADS, HEAD_DIM)),
        'cache_c_v': nrm(ks[8], (DEC_BATCH, N_C, PAST_LEN, KV_HEADS, HEAD_DIM)),
        'c_ctx': nrm(ks[9], (d,)),
        'w_ada': nrm(ks[10], (DEPTH, d, 6 * d), 0.5 * d ** -0.5),
        'b_ada': nrm(ks[11], (DEPTH, 6 * d), 0.02),
        'norm_g': gain(ks[12], (DEPTH, 2, d)),
        'w_mlp_in': nrm(ks[13], (DEPTH, d, D_FF), d ** -0.5),
        'w_mlp_out': nrm(ks[14], (DEPTH, D_FF, d), D_FF ** -0.5),
        'a_w_qkv': nrm(ks[15], (N_A, d, QKV_DIM), d ** -0.5),
        'a_sink': nrm(ks[16], (N_A, N_HEADS), 0.5),
        'a_w_o': nrm(ks[17], (N_A, N_HEADS * HEAD_DIM, d), (N_HEADS * HEAD_DIM) ** -0.5),
        'b_w_dq': nrm(ks[18], (N_B, d, Q_LORA), d ** -0.5),
        'b_g_q': gain(ks[19], (N_B, Q_LORA)),
        'b_w_uq': nrm(ks[20], (N_B, Q_LORA, N_HEADS * (NOPE_DIM + ROPE_DIM)), Q_LORA ** -0.5),
        'b_w_dkv': nrm(ks[21], (N_B, d, KV_LORA + ROPE_DIM), d ** -0.5),
        'b_g_kv': gain(ks[22], (N_B, KV_LORA)),
        'b_w_ukv': nrm(ks[23], (N_B, KV_LORA, N_HEADS * (NOPE_DIM + V_DIM)), KV_LORA ** -0.5),
        'b_w_o': nrm(ks[24], (N_B, N_HEADS * V_DIM, d), (N_HEADS * V_DIM) ** -0.5),
        'c_w_qkv': nrm(ks[25], (N_C, d, QKV_DIM), d ** -0.5),
        'c_g_q': gain(ks[26], (N_C, HEAD_DIM)),
        'c_g_k': gain(ks[27], (N_C, HEAD_DIM)),
        'c_w_o': nrm(ks[28], (N_C, N_HEADS * HEAD_DIM, d), (N_HEADS * HEAD_DIM) ** -0.5),
        'g_final': gain(ks[29], (d,)),
    }


def reference(x_prompt, x_sample, c, cache_a_k, cache_a_v, cache_b_ckv, cache_b_krope, cache_c_k, cache_c_v,
              c_ctx, w_ada, b_ada, norm_g, w_mlp_in, w_mlp_out,
              a_w_qkv, a_sink, a_w_o,
              b_w_dq, b_g_q, b_w_uq, b_w_dkv, b_g_kv, b_w_ukv, b_w_o,
              c_w_qkv, c_g_q, c_g_k, c_w_o, g_final):
    rows, cols = grid_positions(x_sample.shape[1])
    xp, xs = x_prompt, x_sample
    st_a_k, st_a_v, st_b_ckv, st_b_kr, st_c_k, st_c_v = [], [], [], [], [], []
    for i in range(DEPTH):
        kind = i % N_MIXERS
        j = i // N_MIXERS
        sh_p, sc_p, gt_p, sh2_p, sc2_p, gt2_p = ada_mod(c_ctx, w_ada[i], b_ada[i])
        sh_s, sc_s, gt_s, sh2_s, sc2_s, gt2_s = ada_mod(c, w_ada[i], b_ada[i])
        hp = modulate(rmsnorm(xp, norm_g[i, 0]), sh_p, sc_p)
        hs = modulate(rmsnorm(xs, norm_g[i, 0]), sh_s, sc_s)
        if kind == 0:
            op, k_ctx, v_ctx = mixer_a_context(hp, a_w_qkv[j], a_sink[j], a_w_o[j])
            st_a_k.append(k_ctx)
            st_a_v.append(v_ctx)
            os_ = mixer_a_latent(hs, cache_a_k[:, j], cache_a_v[:, j], a_w_qkv[j], a_sink[j], a_w_o[j], rows, cols)
        elif kind == 1:
            op, ckv_ctx, kr_ctx = mixer_b_context(hp, b_w_dq[j], b_g_q[j], b_w_uq[j], b_w_dkv[j], b_g_kv[j],
                                                  b_w_ukv[j], b_w_o[j])
            st_b_ckv.append(ckv_ctx)
            st_b_kr.append(kr_ctx)
            os_ = mixer_b_latent(hs, cache_b_ckv[:, j], cache_b_krope[:, j], b_w_dq[j], b_g_q[j], b_w_uq[j],
                                 b_w_dkv[j], b_g_kv[j], b_w_ukv[j], b_w_o[j], rows, cols)
        else:
            op, k_ctx, v_ctx = mixer_c_context(hp, c_w_qkv[j], c_g_q[j], c_g_k[j], c_w_o[j])
            st_c_k.append(k_ctx)
            st_c_v.append(v_ctx)
            os_ = mixer_c_latent(hs, cache_c_k[:, j], cache_c_v[:, j], c_w_qkv[j], c_g_q[j], c_g_k[j], c_w_o[j],
                                 rows, cols)
        xp = xp + gt_p * op
        xs = xs + gt_s * os_
        hp = modulate(rmsnorm(xp, norm_g[i, 1]), sh2_p, sc2_p)
        hs = modulate(rmsnorm(xs, norm_g[i, 1]), sh2_s, sc2_s)
        xp = xp + gt2_p * sq_relu_mlp(hp, w_mlp_in[i], w_mlp_out[i])
        xs = xs + gt2_s * sq_relu_mlp(hs, w_mlp_in[i], w_mlp_out[i])
    y_prompt = rmsnorm(xp, g_final)
    y_sample = rmsnorm(xs, g_final)
    state_a_k = jnp.stack(st_a_k, axis=1)
    state_a_v = jnp.stack(st_a_v, axis=1)
    state_b_ckv = jnp.stack(st_b_ckv, axis=1)
    state_b_krope = jnp.stack(st_b_kr, axis=1)
    state_c_k = jnp.stack(st_c_k, axis=1)
    state_c_v = jnp.stack(st_c_v, axis=1)
    return (y_prompt, y_sample, state_a_k, state_a_v, state_b_ckv, state_b_krope, state_c_k, state_c_v)
```

```python
import functools

import jax
import jax.numpy as jnp
from jax import lax
from jax.experimental import pallas as pl
from jax.experimental.pallas import tpu as pltpu

D_MODEL = 1024
DEPTH = 4
N_HEADS = 16
KV_HEADS = 4
GROUP = N_HEADS // KV_HEADS
HEAD_DIM = 64
QKV_DIM = (N_HEADS + 2 * KV_HEADS) * HEAD_DIM
Q_LORA = 384
KV_LORA = 256
NOPE_DIM = 64
ROPE_DIM = 32
V_DIM = 64
D_FF = 4 * D_MODEL
GRID_W = 64
WINDOW_BLOCK = 128
ROPE_THETA = 10000.0
EPS = 1e-6
NEG_INF = -1e30
ATTN_SCALE = HEAD_DIM ** -0.5
MLA_SCALE = (NOPE_DIM + ROPE_DIM) ** -0.5

MLA_HEAD_PAD = 128
CKR_PAD = KV_LORA + 128
ROW_TILE = 256
KV_TILE = 256
FF_CHUNK = 1024
V7X_VMEM_LIMIT_BYTES = 48 * 1024 * 1024

F32 = jnp.float32
BF16 = jnp.bfloat16


def _params(n_axes):
    return pltpu.CompilerParams(dimension_semantics=("arbitrary",) * n_axes,
                                vmem_limit_bytes=V7X_VMEM_LIMIT_BYTES)


def _dot(a, b):
    return jnp.dot(a, b, preferred_element_type=F32)


def _rmsnorm(x, g):
    ms = jnp.mean(x * x, axis=-1, keepdims=True)
    return x * lax.rsqrt(ms + EPS) * g


def _norm_mod(x, g, shift, scale):
    return _rmsnorm(x, g) * (1.0 + scale) + shift


def _ada_kernel(cond_ref, w_ref, b_ref, o_ref):
    cnd = cond_ref[...]
    act = cnd / (1.0 + jnp.exp(-cnd))
    o_ref[0] = _dot(act.astype(BF16), w_ref[0].astype(BF16)) + b_ref[0]


def _ada_all(cond, w_ada, b_ada):
    tn = 1536
    n = 6 * D_MODEL
    return pl.pallas_call(
        _ada_kernel,
        grid=(DEPTH, n // tn),
        in_specs=[pl.BlockSpec((8, D_MODEL), lambda l, j: (0, 0)),
                  pl.BlockSpec((1, D_MODEL, tn), lambda l, j: (l, 0, j)),
                  pl.BlockSpec((1, 1, tn), lambda l, j: (l, 0, j))],
        out_specs=pl.BlockSpec((1, 8, tn), lambda l, j: (l, 0, j)),
        out_shape=jax.ShapeDtypeStruct((DEPTH, 8, n), F32),
        compiler_params=_params(2),
        name="ada_mod",
    )(cond, w_ada, b_ada.reshape(DEPTH, 1, n))


def _rope_t(x3, tab, half):
    a, b = x3[:, 0:half], x3[:, half:2 * half]
    c, d = x3[:, 2 * half:3 * half], x3[:, 3 * half:4 * half]
    cr, sr, cc, sc = tab[0], tab[1], tab[2], tab[3]
    return jnp.concatenate([a * cr - b * sr, a * sr + b * cr, c * cc - d * sc, c * sc + d * cc], axis=1)


def _rope_tables(t_len, dim):
    half = dim // 2
    pos = jnp.arange(t_len)
    rows = (pos // GRID_W).astype(F32)
    cols = (pos % GRID_W).astype(F32)
    freqs = ROPE_THETA ** (-jnp.arange(0, half, 2, dtype=F32) / half)
    ang_r = freqs[:, None] * rows[None, :]
    ang_c = freqs[:, None] * cols[None, :]
    return jnp.stack([jnp.cos(ang_r), jnp.sin(ang_r), jnp.cos(ang_c), jnp.sin(ang_c)])


def _proj_gqa_kernel(*refs, rope, qknorm, states):
    x_ref, mod_ref, g_ref, w_ref = refs[:4]
    pos = 4
    if rope:
        tab_ref = refs[pos]
        pos += 1
    if qknorm:
        gq_ref, gk_ref = refs[pos], refs[pos + 1]
        pos += 2
    qT_ref, k_ref, vT_ref = refs[pos:pos + 3]
    pos += 3
    tm = x_ref.shape[1]
    nq = N_HEADS * HEAD_DIM
    nk = KV_HEADS * HEAD_DIM

    mod = mod_ref[0]
    h = _norm_mod(x_ref[0], g_ref[...], mod[:, 0:D_MODEL], mod[:, D_MODEL:2 * D_MODEL])
    y = _dot(h.astype(BF16), w_ref[...])
    q3 = y[:, :nq].T.reshape(N_HEADS, HEAD_DIM, tm)
    k3 = y[:, nq:nq + nk].T.reshape(KV_HEADS, HEAD_DIM, tm)
    v = y[:, nq + nk:]
    if qknorm:
        q3 = q3 * lax.rsqrt(jnp.mean(q3 * q3, axis=1, keepdims=True) + EPS) * gq_ref[...]
        k3 = k3 * lax.rsqrt(jnp.mean(k3 * k3, axis=1, keepdims=True) + EPS) * gk_ref[...]
    if states:
        ks_ref, vs_ref = refs[pos], refs[pos + 1]
        ks_ref[0] = k3.reshape(nk, tm).T if qknorm else y[:, nq:nq + nk]
        vs_ref[0] = v
    if rope:
        tab = tab_ref[...]
        q3 = _rope_t(q3, tab, HEAD_DIM // 4)
        k3 = _rope_t(k3, tab, HEAD_DIM // 4)
    qT_ref[0] = (q3 * ATTN_SCALE).reshape(nq, tm).astype(BF16)
    k_ref[0] = k3.reshape(nk, tm).T.astype(BF16)
    vT_ref[0, 0] = v.T.astype(BF16)


def _proj_gqa(x, mods, mod_row, g, w, tab, gq, gk, *, states):
    b, t, _ = x.shape
    tm = ROW_TILE
    nq = N_HEADS * HEAD_DIM
    nk = KV_HEADS * HEAD_DIM
    rope = tab is not None
    qknorm = gq is not None
    in_specs = [pl.BlockSpec((1, tm, D_MODEL), lambda i, j: (i, j, 0)),
                pl.BlockSpec((1, 1, 6 * D_MODEL), lambda i, j: (mod_row(i), 0, 0)),
                pl.BlockSpec((1, D_MODEL), lambda i, j: (0, 0)),
                pl.BlockSpec((D_MODEL, QKV_DIM), lambda i, j: (0, 0))]
    args = [x, mods, g.reshape(1, D_MODEL), w.astype(BF16)]
    if rope:
        in_specs.append(pl.BlockSpec((4, HEAD_DIM // 4, tm), lambda i, j: (0, 0, j)))
        args.append(tab)
    if qknorm:
        in_specs += [pl.BlockSpec((HEAD_DIM, 1), lambda i, j: (0, 0))] * 2
        args += [gq.reshape(HEAD_DIM, 1), gk.reshape(HEAD_DIM, 1)]
    out_specs = [pl.BlockSpec((1, nq, tm), lambda i, j: (i, 0, j)),
                 pl.BlockSpec((1, tm, nk), lambda i, j: (i, j, 0)),
                 pl.BlockSpec((1, 1, nk, tm), lambda i, j: (i, j, 0, 0))]
    out_shape = [jax.ShapeDtypeStruct((b, nq, t), BF16),
                 jax.ShapeDtypeStruct((b, t, nk), BF16),
                 jax.ShapeDtypeStruct((b, t // tm, nk, tm), BF16)]
    if states:
        out_specs += [pl.BlockSpec((1, tm, nk), lambda i, j: (i, j, 0))] * 2
        out_shape += [jax.ShapeDtypeStruct((b, t, nk), F32)] * 2
    return pl.pallas_call(
        functools.partial(_proj_gqa_kernel, rope=rope, qknorm=qknorm, states=states),
        grid=(b, t // tm), in_specs=in_specs, out_specs=out_specs, out_shape=out_shape,
        compiler_params=_params(2), name="proj_gqa",
    )(*args)


def _proj_mla_kernel(*refs, rope, states):
    x_ref, mod_ref, g_ref, wdq_ref, gq_ref, wuq_ref, wdkv_ref, gkv_ref = refs[:8]
    pos = 8
    if rope:
        tab_ref = refs[pos]
        pos += 1
    qT_ref, ck_ref = refs[pos], refs[pos + 1]
    pos += 2
    tm = x_ref.shape[1]
    quarter = ROPE_DIM // 4

    mod = mod_ref[0]
    h = _norm_mod(x_ref[0], g_ref[...], mod[:, 0:D_MODEL], mod[:, D_MODEL:2 * D_MODEL]).astype(BF16)
    qn = _rmsnorm(_dot(h, wdq_ref[...]), gq_ref[...])
    q3 = _dot(qn.astype(BF16), wuq_ref[...]).T.reshape(N_HEADS, MLA_HEAD_PAD, tm)
    ckv = _dot(h, wdkv_ref[...])
    cn = _rmsnorm(ckv[:, :KV_LORA], gkv_ref[...])
    kr = ckv[:, KV_LORA:]
    if states:
        cs_ref, krs_ref = refs[pos], refs[pos + 1]
        cs_ref[0] = cn
        krs_ref[0] = kr[:, :ROPE_DIM]
    if rope:
        tab = tab_ref[...]
        q_rot = _rope_t(q3[:, NOPE_DIM:NOPE_DIM + ROPE_DIM], tab, quarter)
        q3 = jnp.concatenate([q3[:, :NOPE_DIM], q_rot, q3[:, NOPE_DIM + ROPE_DIM:]], axis=1)
        krT = kr.T
        kr_rot = _rope_t(krT[:ROPE_DIM].reshape(1, ROPE_DIM, tm), tab, quarter).reshape(ROPE_DIM, tm)
        kr = jnp.concatenate([kr_rot, krT[ROPE_DIM:]], axis=0).T
    qT_ref[0] = (q3 * MLA_SCALE).reshape(N_HEADS * MLA_HEAD_PAD, tm).astype(BF16)
    ck_ref[0] = jnp.concatenate([cn, kr], axis=1).astype(BF16)


def _proj_mla(x, mods, mod_row, g, wdq, gq, wuq_pad, wdkv_pad, gkv, tab, *, states):
    b, t, _ = x.shape
    tm = ROW_TILE
    rope = tab is not None
    nqp = N_HEADS * MLA_HEAD_PAD
    const = lambda i, j: (0, 0)
    in_specs = [pl.BlockSpec((1, tm, D_MODEL), lambda i, j: (i, j, 0)),
                pl.BlockSpec((1, 1, 6 * D_MODEL), lambda i, j: (mod_row(i), 0, 0)),
                pl.BlockSpec((1, D_MODEL), const),
                pl.BlockSpec((D_MODEL, Q_LORA), const),
                pl.BlockSpec((1, Q_LORA), const),
                pl.BlockSpec((Q_LORA, nqp), const),
                pl.BlockSpec((D_MODEL, CKR_PAD), const),
                pl.BlockSpec((1, KV_LORA), const)]
    args = [x, mods, g.reshape(1, D_MODEL), wdq.astype(BF16), gq.reshape(1, Q_LORA), wuq_pad,
            wdkv_pad, gkv.reshape(1, KV_LORA)]
    if rope:
        in_specs.append(pl.BlockSpec((4, ROPE_DIM // 4, tm), lambda i, j: (0, 0, j)))
        args.append(tab)
    out_specs = [pl.BlockSpec((1, nqp, tm), lambda i, j: (i, 0, j)),
                 pl.BlockSpec((1, tm, CKR_PAD), lambda i, j: (i, j, 0))]
    out_shape = [jax.ShapeDtypeStruct((b, nqp, t), BF16),
                 jax.ShapeDtypeStruct((b, t, CKR_PAD), BF16)]
    if states:
        out_specs += [pl.BlockSpec((1, tm, KV_LORA), lambda i, j: (i, j, 0)),
                      pl.BlockSpec((1, tm, ROPE_DIM), lambda i, j: (i, j, 0))]
        out_shape += [jax.ShapeDtypeStruct((b, t, KV_LORA), F32),
                      jax.ShapeDtypeStruct((b, t, ROPE_DIM), F32)]
    return pl.pallas_call(
        functools.partial(_proj_mla_kernel, rope=rope, states=states),
        grid=(b, t // tm), in_specs=in_specs, out_specs=out_specs, out_shape=out_shape,
        compiler_params=_params(2), name="proj_mla",
    )(*args)


def _expand_kernel(ck_ref, wk_ref, wv_ref, k_ref, vT_ref):
    ck = ck_ref[0]
    k_ref[0] = _dot(ck, wk_ref[...]).astype(BF16)
    vT_ref[0, 0] = _dot(ck[:, :KV_LORA], wv_ref[...]).T.astype(BF16)


def _expand_mla(ck, wk_exp, wv):
    b, t, _ = ck.shape
    tm = ROW_TILE
    nkp = N_HEADS * MLA_HEAD_PAD
    nv = N_HEADS * V_DIM
    return pl.pallas_call(
        _expand_kernel,
        grid=(b, t // tm),
        in_specs=[pl.BlockSpec((1, tm, CKR_PAD), lambda i, j: (i, j, 0)),
                  pl.BlockSpec((CKR_PAD, nkp), lambda i, j: (0, 0)),
                  pl.BlockSpec((KV_LORA, nv), lambda i, j: (0, 0))],
        out_specs=[pl.BlockSpec((1, tm, nkp), lambda i, j: (i, j, 0)),
                   pl.BlockSpec((1, 1, nv, tm), lambda i, j: (i, j, 0, 0))],
        out_shape=[jax.ShapeDtypeStruct((b, t, nkp), BF16),
                   jax.ShapeDtypeStruct((b, t // tm, nv, tm), BF16)],
        compiler_params=_params(2), name="expand_mla",
    )(ck, wk_exp, wv)


def _q_operand(q_ref, u, lo, width, group):
    qg = jnp.concatenate(
        [q_ref[0, (u * group + g) * HEAD_DIM:(u * group + g + 1) * HEAD_DIM, lo:lo + width] for g in range(group)],
        axis=1)
    z = jnp.zeros_like(qg)
    return jnp.concatenate([qg, z] if u == 0 else [z, qg], axis=0)


def _attn_dense_kernel(*refs, mla, tq, src_lens, has_sink):
    n_src = len(src_lens)
    q_ref = refs[0]
    srcs = [(refs[1 + 2 * s], refs[2 + 2 * s]) for s in range(n_src)]
    pos = 1 + 2 * n_src
    sink_ref = None
    if has_sink:
        sink_ref = refs[pos]
        pos += 1
    o_ref = refs[pos]
    pos += 1
    if mla:
        acc_ref, m_ref, l_ref = refs[pos:]
        group = 1
    else:
        qp_ref, acc_ref, m_ref, l_ref = refs[pos:]
        group = GROUP
        for u in range(2):
            qp_ref[u] = _q_operand(q_ref, u, 0, tq, group)
    ng = group * tq
    tk = KV_TILE

    for u in range(2):
        if has_sink:
            m_ref[u] = sink_ref[u]
            l_ref[u] = jnp.ones((1, ng), F32)
        else:
            m_ref[u] = jnp.full((1, ng), NEG_INF, F32)
            l_ref[u] = jnp.zeros((1, ng), F32)
        acc_ref[u] = jnp.zeros((V_DIM, ng), F32)

    def update(kblk, vblk):
        for u in range(2):
            if mla:
                kp = kblk[:, u * MLA_HEAD_PAD:(u + 1) * MLA_HEAD_PAD]
                qop = q_ref[0, u * MLA_HEAD_PAD:(u + 1) * MLA_HEAD_PAD, :]
            else:
                kp = kblk
                qop = qp_ref[u]
            s = _dot(kp, qop)
            m_prev = m_ref[u]
            m_new = jnp.maximum(m_prev, jnp.max(s, axis=0, keepdims=True))
            p = jnp.exp(s - m_new)
            alpha = jnp.exp(m_prev - m_new)
            l_ref[u] = alpha * l_ref[u] + jnp.sum(p, axis=0, keepdims=True)
            acc_ref[u] = alpha * acc_ref[u] + _dot(vblk[u * V_DIM:(u + 1) * V_DIM, :], p.astype(BF16))
            m_ref[u] = m_new

    for (k_ref, v_ref), n in zip(srcs, src_lens):
        nb = n // tk
        if nb == 1:
            update(k_ref[0], v_ref[0, 0])
        else:
            def body(j, carry, k_ref=k_ref, v_ref=v_ref):
                off = pl.multiple_of(j * tk, tk)
                update(k_ref[0, pl.ds(off, tk), :], v_ref[0, j])
                return carry
            lax.fori_loop(0, nb, body, 0)

    for u in range(2):
        o = acc_ref[u] / l_ref[u]
        if mla:
            o_ref[0, u * V_DIM:(u + 1) * V_DIM, :] = o.astype(BF16)
        else:
            for g in range(group):
                r0 = (u * group + g) * HEAD_DIM
                o_ref[0, r0:r0 + HEAD_DIM, :] = o[:, g * tq:(g + 1) * tq].astype(BF16)


def _attn_dense(qT, srcs, sink_cols, *, mla, tq):
    b, _, t = qT.shape
    tk = KV_TILE
    n_pairs = N_HEADS // 2 if mla else KV_HEADS // 2
    q_rows = 2 * MLA_HEAD_PAD if mla else 2 * GROUP * HEAD_DIM
    o_rows = 2 * V_DIM if mla else 2 * GROUP * HEAD_DIM
    k_cols = 2 * MLA_HEAD_PAD if mla else 2 * HEAD_DIM
    ng = tq if mla else GROUP * tq
    in_specs = [pl.BlockSpec((1, q_rows, tq), lambda i, p, j: (i, p, j))]
    args = [qT]
    lens = []
    for k, vT in srcs:
        n = k.shape[1]
        in_specs += [pl.BlockSpec((1, n, k_cols), lambda i, p, j: (i, 0, p)),
                     pl.BlockSpec((1, n // tk, 2 * V_DIM, tk), lambda i, p, j: (i, 0, p, 0))]
        args += [k, vT]
        lens.append(n)
    if sink_cols is not None:
        in_specs.append(pl.BlockSpec((2, 1, ng), lambda i, p, j: (p, 0, 0)))
        args.append(sink_cols)
    scratch = [pltpu.VMEM((2, V_DIM, ng), F32), pltpu.VMEM((2, 1, ng), F32), pltpu.VMEM((2, 1, ng), F32)]
    if not mla:
        scratch = [pltpu.VMEM((2, 2 * HEAD_DIM, ng), BF16)] + scratch
    return pl.pallas_call(
        functools.partial(_attn_dense_kernel, mla=mla, tq=tq, src_lens=tuple(lens),
                          has_sink=sink_cols is not None),
        grid=(b, n_pairs, t // tq), in_specs=in_specs,
        out_specs=pl.BlockSpec((1, o_rows, tq), lambda i, p, j: (i, p, j)),
        out_shape=jax.ShapeDtypeStruct((b, N_HEADS * V_DIM, t), BF16),
        scratch_shapes=scratch, compiler_params=_params(3),
        name="attn_mla" if mla else "attn_gqa",
    )(*args)


def _attn_window_kernel(q_ref, kc_ref, vc_ref, kp_ref, vp_ref, kcur_ref, vcur_ref, kn_ref, vn_ref,
                        sink_ref, o_ref, *, tq, n_steps):
    wb = WINDOW_BLOCK
    step = pl.program_id(2)
    n_sub = tq // wb
    ng = GROUP * wb
    k_off = lax.broadcasted_iota(jnp.int32, (wb, ng), 0)
    q_off = lax.broadcasted_iota(jnp.int32, (wb, ng), 1) % wb
    keep_prev = k_off >= q_off
    keep_next = k_off <= q_off
    first = step == 0
    last = step == n_steps - 1
    n_ctx = kc_ref.shape[1]

    def kv_block(i):
        if i < 0:
            return kp_ref[0], vp_ref[0, 0]
        if i >= n_sub:
            return kn_ref[0], vn_ref[0, 0]
        c, r = divmod(i * wb, KV_TILE)
        return kcur_ref[0, i * wb:(i + 1) * wb, :], vcur_ref[0, c, :, r:r + wb]

    for i in range(n_sub):
        (k0, v0), (k1, v1), (k2, v2) = kv_block(i - 1), kv_block(i), kv_block(i + 1)
        kcat = jnp.concatenate([kc_ref[0], k0, k1, k2], axis=0)
        vcat = jnp.concatenate([vc_ref[0, 0], v0, v1, v2], axis=1)
        keep0 = jnp.logical_and(keep_prev, jnp.logical_not(first)) if i == 0 else keep_prev
        keep2 = jnp.logical_and(keep_next, jnp.logical_not(last)) if i == n_sub - 1 else keep_next
        for u in range(2):
            s = _dot(kcat, _q_operand(q_ref, u, i * wb, wb, GROUP))
            s = jnp.concatenate([s[:n_ctx],
                                 jnp.where(keep0, s[n_ctx:n_ctx + wb], NEG_INF),
                                 s[n_ctx + wb:n_ctx + 2 * wb],
                                 jnp.where(keep2, s[n_ctx + 2 * wb:], NEG_INF)], axis=0)
            sink = sink_ref[u]
            m = jnp.maximum(sink, jnp.max(s, axis=0, keepdims=True))
            p = jnp.exp(s - m)
            l = jnp.sum(p, axis=0, keepdims=True) + jnp.exp(sink - m)
            o = _dot(vcat[u * V_DIM:(u + 1) * V_DIM, :], p.astype(BF16)) / l
            for g in range(GROUP):
                r0 = (u * GROUP + g) * HEAD_DIM
                o_ref[0, r0:r0 + HEAD_DIM, i * wb:(i + 1) * wb] = o[:, g * wb:(g + 1) * wb].astype(BF16)


def _attn_window(qT, k_ctx, vT_ctx, k, vT, sink_cols, *, tq):
    b, _, t = qT.shape
    wb = WINDOW_BLOCK
    n_ctx = k_ctx.shape[1]
    n_steps = t // tq
    n_sub = tq // wb
    n_wb = t // wb
    per_tile = KV_TILE // wb
    pair_rows = 2 * GROUP * HEAD_DIM
    ng = GROUP * wb

    def prev_blk(j):
        return jnp.maximum(j * n_sub - 1, 0)

    def next_blk(j):
        return jnp.minimum((j + 1) * n_sub, n_wb - 1)

    in_specs = [
        pl.BlockSpec((1, pair_rows, tq), lambda i, p, j: (i, p, j)),
        pl.BlockSpec((1, n_ctx, 2 * HEAD_DIM), lambda i, p, j: (i, 0, p)),
        pl.BlockSpec((1, n_ctx // KV_TILE, 2 * V_DIM, KV_TILE), lambda i, p, j: (i, 0, p, 0)),
        pl.BlockSpec((1, wb, 2 * HEAD_DIM), lambda i, p, j: (i, prev_blk(j), p)),
        pl.BlockSpec((1, 1, 2 * V_DIM, wb), lambda i, p, j: (i, prev_blk(j) // per_tile, p, prev_blk(j) % per_tile)),
        pl.BlockSpec((1, tq, 2 * HEAD_DIM), lambda i, p, j: (i, j, p)),
        pl.BlockSpec((1, tq // KV_TILE, 2 * V_DIM, KV_TILE), lambda i, p, j: (i, j, p, 0)),
        pl.BlockSpec((1, wb, 2 * HEAD_DIM), lambda i, p, j: (i, next_blk(j), p)),
        pl.BlockSpec((1, 1, 2 * V_DIM, wb), lambda i, p, j: (i, next_blk(j) // per_tile, p, next_blk(j) % per_tile)),
        pl.BlockSpec((2, 1, ng), lambda i, p, j: (p, 0, 0)),
    ]
    return pl.pallas_call(
        functools.partial(_attn_window_kernel, tq=tq, n_steps=n_steps),
        grid=(b, KV_HEADS // 2, n_steps), in_specs=in_specs,
        out_specs=pl.BlockSpec((1, pair_rows, tq), lambda i, p, j: (i, p, j)),
        out_shape=jax.ShapeDtypeStruct((b, N_HEADS * HEAD_DIM, t), BF16),
        compiler_params=_params(3), name="attn_window",
    )(qT, k_ctx, vT_ctx, k, vT, k, vT, k, vT, sink_cols)


def _post_kernel(*refs, final):
    x_ref, oT_ref, mod_ref, g2_ref, wo_ref, win_ref, wout_ref = refs[:7]
    pos = 7
    if final:
        gf_ref = refs[pos]
        pos += 1
    out_ref = refs[pos]
    mod = mod_ref[0]
    gate, shift2, scale2, gate2 = (mod[:, c * D_MODEL:(c + 1) * D_MODEL] for c in range(2, 6))
    attn = lax.dot_general(oT_ref[0], wo_ref[...], (((0,), (0,)), ((), ())), preferred_element_type=F32)
    x1 = x_ref[0] + gate * attn
    h2 = _norm_mod(x1, g2_ref[...], shift2, scale2).astype(BF16)
    acc = jnp.zeros_like(x1)
    for c in range(D_FF // FF_CHUNK):
        u = jnp.maximum(_dot(h2, win_ref[:, c * FF_CHUNK:(c + 1) * FF_CHUNK]), 0.0)
        acc = acc + _dot((u * u).astype(BF16), wout_ref[c * FF_CHUNK:(c + 1) * FF_CHUNK, :])
    x2 = x1 + gate2 * acc
    out_ref[0] = _rmsnorm(x2, gf_ref[...]) if final else x2


def _post(x, oT, mods, mod_row, g2, wo, win, wout, g_final):
    b, t, _ = x.shape
    tm = ROW_TILE
    final = g_final is not None
    const = lambda i, j: (0, 0)
    in_specs = [pl.BlockSpec((1, tm, D_MODEL), lambda i, j: (i, j, 0)),
                pl.BlockSpec((1, D_MODEL, tm), lambda i, j: (i, 0, j)),
                pl.BlockSpec((1, 1, 6 * D_MODEL), lambda i, j: (mod_row(i), 0, 0)),
                pl.BlockSpec((1, D_MODEL), const),
                pl.BlockSpec((D_MODEL, D_MODEL), const, pipeline_mode=pl.Buffered(1)),
                pl.BlockSpec((D_MODEL, D_FF), const, pipeline_mode=pl.Buffered(1)),
                pl.BlockSpec((D_FF, D_MODEL), const, pipeline_mode=pl.Buffered(1))]
    args = [x, oT, mods, g2.reshape(1, D_MODEL), wo, win, wout]
    if final:
        in_specs.append(pl.BlockSpec((1, D_MODEL), const))
        args.append(g_final.reshape(1, D_MODEL))
    return pl.pallas_call(
        functools.partial(_post_kernel, final=final),
        grid=(b, t // tm), in_specs=in_specs,
        out_specs=pl.BlockSpec((1, tm, D_MODEL), lambda i, j: (i, j, 0)),
        out_shape=jax.ShapeDtypeStruct((b, t, D_MODEL), F32),
        compiler_params=_params(2), name="post_mlp",
    )(*args)


def _sink_cols(sink, width):
    return jnp.repeat(sink.astype(F32).reshape(KV_HEADS, 1, GROUP), width, axis=2)


def _mla_weights(w_uq, w_dkv, w_ukv):
    wuq_pad = jnp.pad(w_uq.reshape(Q_LORA, N_HEADS, NOPE_DIM + ROPE_DIM),
                      ((0, 0), (0, 0), (0, MLA_HEAD_PAD - NOPE_DIM - ROPE_DIM)))
    wuq_pad = wuq_pad.reshape(Q_LORA, N_HEADS * MLA_HEAD_PAD).astype(BF16)
    wdkv_pad = jnp.pad(w_dkv, ((0, 0), (0, CKR_PAD - KV_LORA - ROPE_DIM))).astype(BF16)
    w3 = w_ukv.reshape(KV_LORA, N_HEADS, NOPE_DIM + V_DIM)
    wk_nope = jnp.pad(w3[:, :, :NOPE_DIM], ((0, 0), (0, 0), (0, MLA_HEAD_PAD - NOPE_DIM)))
    place = jnp.pad(jnp.eye(ROPE_DIM, dtype=F32), ((0, 0), (NOPE_DIM, MLA_HEAD_PAD - NOPE_DIM - ROPE_DIM)))
    place = jnp.broadcast_to(place[:, None, :], (ROPE_DIM, N_HEADS, MLA_HEAD_PAD))
    wk_exp = jnp.concatenate([wk_nope.reshape(KV_LORA, -1), place.reshape(ROPE_DIM, -1),
                              jnp.zeros((CKR_PAD - KV_LORA - ROPE_DIM, N_HEADS * MLA_HEAD_PAD), F32)], axis=0)
    wv = w3[:, :, NOPE_DIM:].reshape(KV_LORA, N_HEADS * V_DIM)
    return wuq_pad, wdkv_pad, wk_exp.astype(BF16), wv.astype(BF16)


def _ctx_gqa(cache_k, cache_v):
    b, p = cache_k.shape[:2]
    k = cache_k.reshape(b, p, KV_HEADS * HEAD_DIM).astype(BF16)
    v = cache_v.reshape(b, p // KV_TILE, KV_TILE, KV_HEADS * HEAD_DIM)
    return k, jnp.swapaxes(v, 2, 3).astype(BF16)


def kernel(x_prompt, x_sample, c, cache_a_k, cache_a_v, cache_b_ckv, cache_b_krope, cache_c_k, cache_c_v, c_ctx, w_ada, b_ada, norm_g, w_mlp_in, w_mlp_out, a_w_qkv, a_sink, a_w_o, b_w_dq, b_g_q, b_w_uq, b_w_dkv, b_g_kv, b_w_ukv, b_w_o, c_w_qkv, c_g_q, c_g_k, c_w_o, g_final):
    n_batch, seq = x_prompt.shape[:2]
    n_dec, dec_seq = x_sample.shape[:2]
    cond = jnp.concatenate([c_ctx[None, :], c, jnp.zeros((8 - 1 - n_dec, D_MODEL), F32)], axis=0)
    mods_all = _ada_all(cond, w_ada, b_ada)
    tab64 = _rope_tables(dec_seq, HEAD_DIM)
    tab32 = _rope_tables(dec_seq, ROPE_DIM)
    prompt_row = lambda i: 0
    sample_row = lambda i: i + 1

    xp, xs = x_prompt, x_sample
    st = {"a_k": [], "a_v": [], "b_ckv": [], "b_kr": [], "c_k": [], "c_v": []}
    for layer in range(DEPTH):
        kind, j = layer % 3, layer // 3
        mods = mods_all[layer].reshape(8, 1, 6 * D_MODEL)
        g1, g2 = norm_g[layer, 0], norm_g[layer, 1]
        if kind == 0:
            sink = a_sink[j]
            qT, k, vT, ks, vs = _proj_gqa(xp, mods, prompt_row, g1, a_w_qkv[j], None, None, None, states=True)
            st["a_k"].append(ks)
            st["a_v"].append(vs)
            oT_p = _attn_dense(qT, [(k, vT)], _sink_cols(sink, seq), mla=False, tq=seq)
            qT, k, vT = _proj_gqa(xs, mods, sample_row, g1, a_w_qkv[j], tab64, None, None, states=False)
            k_ctx, vT_ctx = _ctx_gqa(cache_a_k[:, j], cache_a_v[:, j])
            oT_s = _attn_window(qT, k_ctx, vT_ctx, k, vT, _sink_cols(sink, WINDOW_BLOCK), tq=512)
            wo = a_w_o[j]
        elif kind == 1:
            wuq_pad, wdkv_pad, wk_exp, wv = _mla_weights(b_w_uq[j], b_w_dkv[j], b_w_ukv[j])
            qT, ck, cs, krs = _proj_mla(xp, mods, prompt_row, g1, b_w_dq[j], b_g_q[j], wuq_pad, wdkv_pad,
                                        b_g_kv[j], None, states=True)
            st["b_ckv"].append(cs)
            st["b_kr"].append(krs)
            oT_p = _attn_dense(qT, [_expand_mla(ck, wk_exp, wv)], None, mla=True, tq=seq)
            qT, ck = _proj_mla(xs, mods, sample_row, g1, b_w_dq[j], b_g_q[j], wuq_pad, wdkv_pad,
                               b_g_kv[j], tab32, states=False)
            ck_ctx = jnp.concatenate(
                [cache_b_ckv[:, j], cache_b_krope[:, j],
                 jnp.zeros(cache_b_krope.shape[:1] + cache_b_krope.shape[2:3] + (CKR_PAD - KV_LORA - ROPE_DIM,), F32)],
                axis=-1).astype(BF16)
            oT_s = _attn_dense(qT, [_expand_mla(ck_ctx, wk_exp, wv), _expand_mla(ck, wk_exp, wv)], None,
                               mla=True, tq=1024)
            wo = b_w_o[j]
        else:
            qT, k, vT, ks, vs = _proj_gqa(xp, mods, prompt_row, g1, c_w_qkv[j], None, c_g_q[j], c_g_k[j],
                                          states=True)
            st["c_k"].append(ks)
            st["c_v"].append(vs)
            oT_p = _attn_dense(qT, [(k, vT)], None, mla=False, tq=seq)
            qT, k, vT = _proj_gqa(xs, mods, sample_row, g1, c_w_qkv[j], tab64, c_g_q[j], c_g_k[j], states=False)
            oT_s = _attn_dense(qT, [_ctx_gqa(cache_c_k[:, j], cache_c_v[:, j]), (k, vT)], None,
                               mla=False, tq=512)
            wo = c_w_o[j]
        gf = g_final if layer == DEPTH - 1 else None
        wo, win, wout = wo.astype(BF16), w_mlp_in[layer].astype(BF16), w_mlp_out[layer].astype(BF16)
        xp = _post(xp, oT_p, mods, prompt_row, g2, wo, win, wout, gf)
        xs = _post(xs, oT_s, mods, sample_row, g2, wo, win, wout, gf)

    def stack_heads(parts):
        return jnp.stack([p.reshape(n_batch, seq, KV_HEADS, HEAD_DIM) for p in parts], axis=1)

    return (xp, xs, stack_heads(st["a_k"]), stack_heads(st["a_v"]),
            jnp.stack(st["b_ckv"], axis=1), jnp.stack(st["b_kr"], axis=1),
            stack_heads(st["c_k"]), stack_heads(st["c_v"]))
```

```python
import functools

import jax
import jax.numpy as jnp
from jax import lax
from jax.experimental import pallas as pl
from jax.experimental.pallas import tpu as pltpu

D_MODEL = 1024
DEPTH = 4
N_HEADS = 16
KV_HEADS = 4
GROUP = N_HEADS // KV_HEADS
HEAD_DIM = 64
QKV_DIM = (N_HEADS + 2 * KV_HEADS) * HEAD_DIM
Q_LORA = 384
KV_LORA = 256
NOPE_DIM = 64
ROPE_DIM = 32
V_DIM = 64
D_FF = 4 * D_MODEL
GRID_W = 64
WINDOW_BLOCK = 128
ROPE_THETA = 10000.0
EPS = 1e-6
NEG_INF = -1e30
LOG2_E = 1.4426950408889634
ATTN_Q_SCALE = HEAD_DIM ** -0.5 * LOG2_E
MLA_Q_SCALE = (NOPE_DIM + ROPE_DIM) ** -0.5 * LOG2_E

MLA_HEAD_PAD = 128
CKR_PAD = KV_LORA + 128
ROW_TILE = 256
KV_TILE = 256
SOFTMAX_CHUNK = 256
FF_CHUNK = 1024
V7X_VMEM_LIMIT_BYTES = 48 * 1024 * 1024

F32 = jnp.float32
BF16 = jnp.bfloat16


def _params(n_axes):
    return pltpu.CompilerParams(dimension_semantics=("arbitrary",) * n_axes,
                                vmem_limit_bytes=V7X_VMEM_LIMIT_BYTES)


def _dot(a, b):
    return jnp.dot(a, b, preferred_element_type=F32)


def _rmsnorm(x, g):
    ms = jnp.mean(x * x, axis=-1, keepdims=True)
    return x * lax.rsqrt(ms + EPS) * g


def _norm_mod(x, g, shift, scale):
    return _rmsnorm(x, g) * (1.0 + scale) + shift


def _ada_kernel(cond_ref, w_ref, b_ref, o_ref):
    cnd = cond_ref[...]
    act = cnd / (1.0 + jnp.exp(-cnd))
    o_ref[0] = _dot(act.astype(BF16), w_ref[0].astype(BF16)) + b_ref[0]


def _ada_all(cond, w_ada, b_ada):
    tn = 1536
    n = 6 * D_MODEL
    return pl.pallas_call(
        _ada_kernel,
        grid=(DEPTH, n // tn),
        in_specs=[pl.BlockSpec((8, D_MODEL), lambda l, j: (0, 0)),
                  pl.BlockSpec((1, D_MODEL, tn), lambda l, j: (l, 0, j)),
                  pl.BlockSpec((1, 1, tn), lambda l, j: (l, 0, j))],
        out_specs=pl.BlockSpec((1, 8, tn), lambda l, j: (l, 0, j)),
        out_shape=jax.ShapeDtypeStruct((DEPTH, 8, n), F32),
        compiler_params=_params(2),
        name="ada_mod",
    )(cond, w_ada, b_ada.reshape(DEPTH, 1, n))


def _rope_t(x3, tab, half):
    a, b = x3[:, 0:half], x3[:, half:2 * half]
    c, d = x3[:, 2 * half:3 * half], x3[:, 3 * half:4 * half]
    cr, sr, cc, sc = tab[0], tab[1], tab[2], tab[3]
    return jnp.concatenate([a * cr - b * sr, a * sr + b * cr, c * cc - d * sc, c * sc + d * cc], axis=1)


def _rope_tables(t_len, dim):
    half = dim // 2
    pos = jnp.arange(t_len)
    rows = (pos // GRID_W).astype(F32)
    cols = (pos % GRID_W).astype(F32)
    freqs = ROPE_THETA ** (-jnp.arange(0, half, 2, dtype=F32) / half)
    ang_r = freqs[:, None] * rows[None, :]
    ang_c = freqs[:, None] * cols[None, :]
    return jnp.stack([jnp.cos(ang_r), jnp.sin(ang_r), jnp.cos(ang_c), jnp.sin(ang_c)])


def _proj_gqa_kernel(*refs, rope, qknorm, states):
    x_ref, mod_ref, g_ref, w_ref = refs[:4]
    pos = 4
    if rope:
        tab_ref = refs[pos]
        pos += 1
    if qknorm:
        gq_ref, gk_ref = refs[pos], refs[pos + 1]
        pos += 2
    qT_ref, k_ref, vT_ref = refs[pos:pos + 3]
    pos += 3
    tm = x_ref.shape[1]
    nq = N_HEADS * HEAD_DIM
    nk = KV_HEADS * HEAD_DIM

    mod = mod_ref[0]
    h = _norm_mod(x_ref[0], g_ref[...], mod[:, 0:D_MODEL], mod[:, D_MODEL:2 * D_MODEL])
    y = _dot(h.astype(BF16), w_ref[...])
    q3 = y[:, :nq].T.reshape(N_HEADS, HEAD_DIM, tm)
    k3 = y[:, nq:nq + nk].T.reshape(KV_HEADS, HEAD_DIM, tm)
    v = y[:, nq + nk:]
    if qknorm:
        q3 = q3 * lax.rsqrt(jnp.mean(q3 * q3, axis=1, keepdims=True) + EPS) * gq_ref[...]
        k3 = k3 * lax.rsqrt(jnp.mean(k3 * k3, axis=1, keepdims=True) + EPS) * gk_ref[...]
    if states:
        ks_ref, vs_ref = refs[pos], refs[pos + 1]
        ks_ref[0] = k3.reshape(nk, tm).T if qknorm else y[:, nq:nq + nk]
        vs_ref[0] = v
    if rope:
        tab = tab_ref[...]
        q3 = _rope_t(q3, tab, HEAD_DIM // 4)
        k3 = _rope_t(k3, tab, HEAD_DIM // 4)
    qT_ref[0] = (q3 * ATTN_Q_SCALE).reshape(nq, tm).astype(BF16)
    k_ref[0] = k3.reshape(nk, tm).T.astype(BF16)
    vT_ref[0, 0] = v.T.astype(BF16)


def _proj_gqa(x, mods, mod_row, g, w, tab, gq, gk, *, states):
    b, t, _ = x.shape
    tm = ROW_TILE
    nq = N_HEADS * HEAD_DIM
    nk = KV_HEADS * HEAD_DIM
    rope = tab is not None
    qknorm = gq is not None
    in_specs = [pl.BlockSpec((1, tm, D_MODEL), lambda i, j: (i, j, 0)),
                pl.BlockSpec((1, 1, 6 * D_MODEL), lambda i, j: (mod_row(i), 0, 0)),
                pl.BlockSpec((1, D_MODEL), lambda i, j: (0, 0)),
                pl.BlockSpec((D_MODEL, QKV_DIM), lambda i, j: (0, 0))]
    args = [x, mods, g.reshape(1, D_MODEL), w.astype(BF16)]
    if rope:
        in_specs.append(pl.BlockSpec((4, HEAD_DIM // 4, tm), lambda i, j: (0, 0, j)))
        args.append(tab)
    if qknorm:
        in_specs += [pl.BlockSpec((HEAD_DIM, 1), lambda i, j: (0, 0))] * 2
        args += [gq.reshape(HEAD_DIM, 1), gk.reshape(HEAD_DIM, 1)]
    out_specs = [pl.BlockSpec((1, nq, tm), lambda i, j: (i, 0, j)),
                 pl.BlockSpec((1, tm, nk), lambda i, j: (i, j, 0)),
                 pl.BlockSpec((1, 1, nk, tm), lambda i, j: (i, j, 0, 0))]
    out_shape = [jax.ShapeDtypeStruct((b, nq, t), BF16),
                 jax.ShapeDtypeStruct((b, t, nk), BF16),
                 jax.ShapeDtypeStruct((b, t // tm, nk, tm), BF16)]
    if states:
        out_specs += [pl.BlockSpec((1, tm, nk), lambda i, j: (i, j, 0))] * 2
        out_shape += [jax.ShapeDtypeStruct((b, t, nk), F32)] * 2
    return pl.pallas_call(
        functools.partial(_proj_gqa_kernel, rope=rope, qknorm=qknorm, states=states),
        grid=(b, t // tm), in_specs=in_specs, out_specs=out_specs, out_shape=out_shape,
        compiler_params=_params(2), name="proj_gqa",
    )(*args)


def _proj_mla_kernel(*refs, rope, states):
    x_ref, mod_ref, g_ref, wdq_ref, gq_ref, wuq_ref, wdkv_ref, gkv_ref = refs[:8]
    pos = 8
    if rope:
        tab_ref = refs[pos]
        pos += 1
    qT_ref, ck_ref = refs[pos], refs[pos + 1]
    pos += 2
    tm = x_ref.shape[1]
    quarter = ROPE_DIM // 4

    mod = mod_ref[0]
    h = _norm_mod(x_ref[0], g_ref[...], mod[:, 0:D_MODEL], mod[:, D_MODEL:2 * D_MODEL]).astype(BF16)
    qn = _rmsnorm(_dot(h, wdq_ref[...]), gq_ref[...])
    q3 = _dot(qn.astype(BF16), wuq_ref[...]).T.reshape(N_HEADS, MLA_HEAD_PAD, tm)
    ckv = _dot(h, wdkv_ref[...])
    cn = _rmsnorm(ckv[:, :KV_LORA], gkv_ref[...])
    kr = ckv[:, KV_LORA:]
    if states:
        cs_ref, krs_ref = refs[pos], refs[pos + 1]
        cs_ref[0] = cn
        krs_ref[0] = kr[:, :ROPE_DIM]
    if rope:
        tab = tab_ref[...]
        q_rot = _rope_t(q3[:, NOPE_DIM:NOPE_DIM + ROPE_DIM], tab, quarter)
        q3 = jnp.concatenate([q3[:, :NOPE_DIM], q_rot, q3[:, NOPE_DIM + ROPE_DIM:]], axis=1)
        krT = kr.T
        kr_rot = _rope_t(krT[:ROPE_DIM].reshape(1, ROPE_DIM, tm), tab, quarter).reshape(ROPE_DIM, tm)
        kr = jnp.concatenate([kr_rot, krT[ROPE_DIM:]], axis=0).T
    qT_ref[0] = (q3 * MLA_Q_SCALE).reshape(N_HEADS * MLA_HEAD_PAD, tm).astype(BF16)
    ck_ref[0] = jnp.concatenate([cn, kr], axis=1).astype(BF16)


def _proj_mla(x, mods, mod_row, g, wdq, gq, wuq_pad, wdkv_pad, gkv, tab, *, states):
    b, t, _ = x.shape
    tm = ROW_TILE
    rope = tab is not None
    nqp = N_HEADS * MLA_HEAD_PAD
    const = lambda i, j: (0, 0)
    in_specs = [pl.BlockSpec((1, tm, D_MODEL), lambda i, j: (i, j, 0)),
                pl.BlockSpec((1, 1, 6 * D_MODEL), lambda i, j: (mod_row(i), 0, 0)),
                pl.BlockSpec((1, D_MODEL), const),
                pl.BlockSpec((D_MODEL, Q_LORA), const),
                pl.BlockSpec((1, Q_LORA), const),
                pl.BlockSpec((Q_LORA, nqp), const),
                pl.BlockSpec((D_MODEL, CKR_PAD), const),
                pl.BlockSpec((1, KV_LORA), const)]
    args = [x, mods, g.reshape(1, D_MODEL), wdq.astype(BF16), gq.reshape(1, Q_LORA), wuq_pad,
            wdkv_pad, gkv.reshape(1, KV_LORA)]
    if rope:
        in_specs.append(pl.BlockSpec((4, ROPE_DIM // 4, tm), lambda i, j: (0, 0, j)))
        args.append(tab)
    out_specs = [pl.BlockSpec((1, nqp, tm), lambda i, j: (i, 0, j)),
                 pl.BlockSpec((1, tm, CKR_PAD), lambda i, j: (i, j, 0))]
    out_shape = [jax.ShapeDtypeStruct((b, nqp, t), BF16),
                 jax.ShapeDtypeStruct((b, t, CKR_PAD), BF16)]
    if states:
        out_specs += [pl.BlockSpec((1, tm, KV_LORA), lambda i, j: (i, j, 0)),
                      pl.BlockSpec((1, tm, ROPE_DIM), lambda i, j: (i, j, 0))]
        out_shape += [jax.ShapeDtypeStruct((b, t, KV_LORA), F32),
                      jax.ShapeDtypeStruct((b, t, ROPE_DIM), F32)]
    return pl.pallas_call(
        functools.partial(_proj_mla_kernel, rope=rope, states=states),
        grid=(b, t // tm), in_specs=in_specs, out_specs=out_specs, out_shape=out_shape,
        compiler_params=_params(2), name="proj_mla",
    )(*args)


def _expand_kernel(ck_ref, wk_ref, wv_ref, k_ref, vT_ref):
    ck = ck_ref[0]
    k_ref[0] = _dot(ck, wk_ref[...]).astype(BF16)
    vT_ref[0, 0] = _dot(ck[:, :KV_LORA], wv_ref[...]).T.astype(BF16)


def _expand_mla(ck, wk_exp, wv):
    b, t, _ = ck.shape
    tm = ROW_TILE
    nkp = N_HEADS * MLA_HEAD_PAD
    nv = N_HEADS * V_DIM
    return pl.pallas_call(
        _expand_kernel,
        grid=(b, t // tm),
        in_specs=[pl.BlockSpec((1, tm, CKR_PAD), lambda i, j: (i, j, 0)),
                  pl.BlockSpec((CKR_PAD, nkp), lambda i, j: (0, 0)),
                  pl.BlockSpec((KV_LORA, nv), lambda i, j: (0, 0))],
        out_specs=[pl.BlockSpec((1, tm, nkp), lambda i, j: (i, j, 0)),
                   pl.BlockSpec((1, 1, nv, tm), lambda i, j: (i, j, 0, 0))],
        out_shape=[jax.ShapeDtypeStruct((b, t, nkp), BF16),
                   jax.ShapeDtypeStruct((b, t // tm, nv, tm), BF16)],
        compiler_params=_params(2), name="expand_mla",
    )(ck, wk_exp, wv)


def _q_operand(q_ref, u, lo, width, group):
    qg = jnp.concatenate(
        [q_ref[0, (u * group + g) * HEAD_DIM:(u * group + g + 1) * HEAD_DIM, lo:lo + width] for g in range(group)],
        axis=1)
    z = jnp.zeros_like(qg)
    return jnp.concatenate([qg, z] if u == 0 else [z, qg], axis=0)


def _attn_dense_kernel(*refs, mla, tq, n_blocks, has_sink):
    q_ref, k_ref, v_ref = refs[:3]
    pos = 3
    sink_ref = None
    if has_sink:
        sink_ref = refs[pos]
        pos += 1
    o_ref = refs[pos]
    pos += 1
    if mla:
        s_ref, mb_ref, p_ref, acc_ref, m_ref, l_ref = refs[pos:]
        group = 1
    else:
        qp_ref, s_ref, mb_ref, p_ref, acc_ref, m_ref, l_ref = refs[pos:]
        group = GROUP
        for u in range(2):
            qp_ref[u] = _q_operand(q_ref, u, 0, tq, group)
    ng = group * tq
    tk = KV_TILE
    cw = SOFTMAX_CHUNK

    for u in range(2):
        if has_sink:
            m_ref[u] = sink_ref[u]
            l_ref[u] = jnp.ones((1, ng), F32)
        else:
            m_ref[u] = jnp.full((1, ng), NEG_INF, F32)
            l_ref[u] = jnp.zeros((1, ng), F32)
        acc_ref[u] = jnp.zeros((V_DIM, ng), F32)

    def scores(j, par):
        off = j * tk if isinstance(j, int) else pl.multiple_of(j * tk, tk)
        for u in range(2):
            if mla:
                kp = k_ref[0, pl.ds(off, tk), u * MLA_HEAD_PAD:(u + 1) * MLA_HEAD_PAD]
                qop = q_ref[0, u * MLA_HEAD_PAD:(u + 1) * MLA_HEAD_PAD, :]
            else:
                kp = k_ref[0, pl.ds(off, tk), :]
                qop = qp_ref[u]
            s = _dot(kp, qop)
            s_ref[u, par] = s
            mb_ref[u, par] = jnp.max(s, axis=0, keepdims=True)

    def accumulate(j, par):
        for u in range(2):
            m_prev = m_ref[u]
            m_new = jnp.maximum(m_prev, mb_ref[u, par])
            alpha = jnp.exp2(m_prev - m_new)
            for c in range(ng // cw):
                cols = slice(c * cw, (c + 1) * cw)
                p = jnp.exp2(s_ref[u, par, :, cols] - m_new[:, cols])
                p_ref[u, :, cols] = p.astype(BF16)
                l_ref[u, :, cols] = alpha[:, cols] * l_ref[u, :, cols] + jnp.sum(p, axis=0, keepdims=True)
            acc_ref[u] = alpha * acc_ref[u] + _dot(v_ref[0, j, u * V_DIM:(u + 1) * V_DIM, :], p_ref[u])
            m_ref[u] = m_new

    scores(0, 0)
    n_loop = (n_blocks - 1) // 2

    def body(i, carry):
        j = 2 * i
        scores(j + 1, 1)
        accumulate(j, 0)
        scores(j + 2, 0)
        accumulate(j + 1, 1)
        return carry

    if n_loop > 0:
        lax.fori_loop(0, n_loop, body, 0)
    if (n_blocks - 1) % 2 == 1:
        scores(n_blocks - 1, 1)
        accumulate(n_blocks - 2, 0)
        accumulate(n_blocks - 1, 1)
    else:
        accumulate(n_blocks - 1, 0)

    for u in range(2):
        o = acc_ref[u] / l_ref[u]
        if mla:
            o_ref[0, u * V_DIM:(u + 1) * V_DIM, :] = o.astype(BF16)
        else:
            for g in range(group):
                r0 = (u * group + g) * HEAD_DIM
                o_ref[0, r0:r0 + HEAD_DIM, :] = o[:, g * tq:(g + 1) * tq].astype(BF16)


def _attn_dense(qT, k, vT, sink_cols, *, mla, tq):
    b, _, t = qT.shape
    tk = KV_TILE
    n = k.shape[1]
    n_pairs = N_HEADS // 2 if mla else KV_HEADS // 2
    q_rows = 2 * MLA_HEAD_PAD if mla else 2 * GROUP * HEAD_DIM
    o_rows = 2 * V_DIM if mla else 2 * GROUP * HEAD_DIM
    k_cols = 2 * MLA_HEAD_PAD if mla else 2 * HEAD_DIM
    ng = tq if mla else GROUP * tq
    in_specs = [pl.BlockSpec((1, q_rows, tq), lambda i, p, j: (i, p, j)),
                pl.BlockSpec((1, n, k_cols), lambda i, p, j: (i, 0, p)),
                pl.BlockSpec((1, n // tk, 2 * V_DIM, tk), lambda i, p, j: (i, 0, p, 0))]
    args = [qT, k, vT]
    if sink_cols is not None:
        in_specs.append(pl.BlockSpec((2, 1, ng), lambda i, p, j: (p, 0, 0)))
        args.append(sink_cols)
    scratch = [pltpu.VMEM((2, 2, tk, ng), F32),
               pltpu.VMEM((2, 2, 1, ng), F32),
               pltpu.VMEM((2, tk, ng), BF16),
               pltpu.VMEM((2, V_DIM, ng), F32),
               pltpu.VMEM((2, 1, ng), F32),
               pltpu.VMEM((2, 1, ng), F32)]
    if not mla:
        scratch = [pltpu.VMEM((2, 2 * HEAD_DIM, ng), BF16)] + scratch
    return pl.pallas_call(
        functools.partial(_attn_dense_kernel, mla=mla, tq=tq, n_blocks=n // tk,
                          has_sink=sink_cols is not None),
        grid=(b, n_pairs, t // tq), in_specs=in_specs,
        out_specs=pl.BlockSpec((1, o_rows, tq), lambda i, p, j: (i, p, j)),
        out_shape=jax.ShapeDtypeStruct((b, N_HEADS * V_DIM, t), BF16),
        scratch_shapes=scratch, compiler_params=_params(3),
        name="attn_mla" if mla else "attn_gqa",
    )(*args)


def _attn_window_kernel(q_ref, kc_ref, vc_ref, kp_ref, vp_ref, kcur_ref, vcur_ref, kn_ref, vn_ref,
                        sink_ref, o_ref, *, tq, n_steps):
    wb = WINDOW_BLOCK
    step = pl.program_id(2)
    n_sub = tq // wb
    ng = GROUP * wb
    k_off = lax.broadcasted_iota(jnp.int32, (wb, ng), 0)
    q_off = lax.broadcasted_iota(jnp.int32, (wb, ng), 1) % wb
    keep_prev = k_off >= q_off
    keep_next = k_off <= q_off
    first = step == 0
    last = step == n_steps - 1
    n_ctx = kc_ref.shape[1]

    def kv_block(i):
        if i < 0:
            return kp_ref[0], vp_ref[0, 0]
        if i >= n_sub:
            return kn_ref[0], vn_ref[0, 0]
        c, r = divmod(i * wb, KV_TILE)
        return kcur_ref[0, i * wb:(i + 1) * wb, :], vcur_ref[0, c, :, r:r + wb]

    for i in range(n_sub):
        (k0, v0), (k1, v1), (k2, v2) = kv_block(i - 1), kv_block(i), kv_block(i + 1)
        kcat = jnp.concatenate([kc_ref[0], k0, k1, k2], axis=0)
        vcat = jnp.concatenate([vc_ref[0, 0], v0, v1, v2], axis=1)
        keep0 = jnp.logical_and(keep_prev, jnp.logical_not(first)) if i == 0 else keep_prev
        keep2 = jnp.logical_and(keep_next, jnp.logical_not(last)) if i == n_sub - 1 else keep_next
        for u in range(2):
            s = _dot(kcat, _q_operand(q_ref, u, i * wb, wb, GROUP))
            s = jnp.concatenate([s[:n_ctx],
                                 jnp.where(keep0, s[n_ctx:n_ctx + wb], NEG_INF),
                                 s[n_ctx + wb:n_ctx + 2 * wb],
                                 jnp.where(keep2, s[n_ctx + 2 * wb:], NEG_INF)], axis=0)
            sink = sink_ref[u]
            m = jnp.maximum(sink, jnp.max(s, axis=0, keepdims=True))
            p = jnp.exp2(s - m)
            l = jnp.sum(p, axis=0, keepdims=True) + jnp.exp2(sink - m)
            o = _dot(vcat[u * V_DIM:(u + 1) * V_DIM, :], p.astype(BF16)) / l
            for g in range(GROUP):
                r0 = (u * GROUP + g) * HEAD_DIM
                o_ref[0, r0:r0 + HEAD_DIM, i * wb:(i + 1) * wb] = o[:, g * wb:(g + 1) * wb].astype(BF16)


def _attn_window(qT, k_ctx, vT_ctx, k, vT, sink_cols, *, tq):
    b, _, t = qT.shape
    wb = WINDOW_BLOCK
    n_ctx = k_ctx.shape[1]
    n_steps = t // tq
    n_sub = tq // wb
    n_wb = t // wb
    per_tile = KV_TILE // wb
    pair_rows = 2 * GROUP * HEAD_DIM
    ng = GROUP * wb

    def prev_blk(j):
        return jnp.maximum(j * n_sub - 1, 0)

    def next_blk(j):
        return jnp.minimum((j + 1) * n_sub, n_wb - 1)

    in_specs = [
        pl.BlockSpec((1, pair_rows, tq), lambda i, p, j: (i, p, j)),
        pl.BlockSpec((1, n_ctx, 2 * HEAD_DIM), lambda i, p, j: (i, 0, p)),
        pl.BlockSpec((1, n_ctx // KV_TILE, 2 * V_DIM, KV_TILE), lambda i, p, j: (i, 0, p, 0)),
        pl.BlockSpec((1, wb, 2 * HEAD_DIM), lambda i, p, j: (i, prev_blk(j), p)),
        pl.BlockSpec((1, 1, 2 * V_DIM, wb), lambda i, p, j: (i, prev_blk(j) // per_tile, p, prev_blk(j) % per_tile)),
        pl.BlockSpec((1, tq, 2 * HEAD_DIM), lambda i, p, j: (i, j, p)),
        pl.BlockSpec((1, tq // KV_TILE, 2 * V_DIM, KV_TILE), lambda i, p, j: (i, j, p, 0)),
        pl.BlockSpec((1, wb, 2 * HEAD_DIM), lambda i, p, j: (i, next_blk(j), p)),
        pl.BlockSpec((1, 1, 2 * V_DIM, wb), lambda i, p, j: (i, next_blk(j) // per_tile, p, next_blk(j) % per_tile)),
        pl.BlockSpec((2, 1, ng), lambda i, p, j: (p, 0, 0)),
    ]
    return pl.pallas_call(
        functools.partial(_attn_window_kernel, tq=tq, n_steps=n_steps),
        grid=(b, KV_HEADS // 2, n_steps), in_specs=in_specs,
        out_specs=pl.BlockSpec((1, pair_rows, tq), lambda i, p, j: (i, p, j)),
        out_shape=jax.ShapeDtypeStruct((b, N_HEADS * HEAD_DIM, t), BF16),
        compiler_params=_params(3), name="attn_window",
    )(qT, k_ctx, vT_ctx, k, vT, k, vT, k, vT, sink_cols)


def _post_kernel(*refs, final):
    x_ref, oT_ref, mod_ref, g2_ref, wo_ref, win_ref, wout_ref = refs[:7]
    pos = 7
    if final:
        gf_ref = refs[pos]
        pos += 1
    out_ref = refs[pos]
    mod = mod_ref[0]
    gate, shift2, scale2, gate2 = (mod[:, c * D_MODEL:(c + 1) * D_MODEL] for c in range(2, 6))
    attn = lax.dot_general(oT_ref[0], wo_ref[...], (((0,), (0,)), ((), ())), preferred_element_type=F32)
    x1 = x_ref[0] + gate * attn
    h2 = _norm_mod(x1, g2_ref[...], shift2, scale2).astype(BF16)
    acc = jnp.zeros_like(x1)
    for c in range(D_FF // FF_CHUNK):
        u = jnp.maximum(_dot(h2, win_ref[:, c * FF_CHUNK:(c + 1) * FF_CHUNK]), 0.0)
        acc = acc + _dot((u * u).astype(BF16), wout_ref[c * FF_CHUNK:(c + 1) * FF_CHUNK, :])
    x2 = x1 + gate2 * acc
    out_ref[0] = _rmsnorm(x2, gf_ref[...]) if final else x2


def _post(x, oT, mods, mod_row, g2, wo, win, wout, g_final):
    b, t, _ = x.shape
    tm = ROW_TILE
    final = g_final is not None
    const = lambda i, j: (0, 0)
    in_specs = [pl.BlockSpec((1, tm, D_MODEL), lambda i, j: (i, j, 0)),
                pl.BlockSpec((1, D_MODEL, tm), lambda i, j: (i, 0, j)),
                pl.BlockSpec((1, 1, 6 * D_MODEL), lambda i, j: (mod_row(i), 0, 0)),
                pl.BlockSpec((1, D_MODEL), const),
                pl.BlockSpec((D_MODEL, D_MODEL), const, pipeline_mode=pl.Buffered(1)),
                pl.BlockSpec((D_MODEL, D_FF), const, pipeline_mode=pl.Buffered(1)),
                pl.BlockSpec((D_FF, D_MODEL), const, pipeline_mode=pl.Buffered(1))]
    args = [x, oT, mods, g2.reshape(1, D_MODEL), wo, win, wout]
    if final:
        in_specs.append(pl.BlockSpec((1, D_MODEL), const))
        args.append(g_final.reshape(1, D_MODEL))
    return pl.pallas_call(
        functools.partial(_post_kernel, final=final),
        grid=(b, t // tm), in_specs=in_specs,
        out_specs=pl.BlockSpec((1, tm, D_MODEL), lambda i, j: (i, j, 0)),
        out_shape=jax.ShapeDtypeStruct((b, t, D_MODEL), F32),
        compiler_params=_params(2), name="post_mlp",
    )(*args)


def _sink_cols(sink, width):
    return jnp.repeat((sink.astype(F32) * LOG2_E).reshape(KV_HEADS, 1, GROUP), width, axis=2)


def _mla_weights(w_uq, w_dkv, w_ukv):
    wuq_pad = jnp.pad(w_uq.reshape(Q_LORA, N_HEADS, NOPE_DIM + ROPE_DIM),
                      ((0, 0), (0, 0), (0, MLA_HEAD_PAD - NOPE_DIM - ROPE_DIM)))
    wuq_pad = wuq_pad.reshape(Q_LORA, N_HEADS * MLA_HEAD_PAD).astype(BF16)
    wdkv_pad = jnp.pad(w_dkv, ((0, 0), (0, CKR_PAD - KV_LORA - ROPE_DIM))).astype(BF16)
    w3 = w_ukv.reshape(KV_LORA, N_HEADS, NOPE_DIM + V_DIM)
    wk_nope = jnp.pad(w3[:, :, :NOPE_DIM], ((0, 0), (0, 0), (0, MLA_HEAD_PAD - NOPE_DIM)))
    place = jnp.pad(jnp.eye(ROPE_DIM, dtype=F32), ((0, 0), (NOPE_DIM, MLA_HEAD_PAD - NOPE_DIM - ROPE_DIM)))
    place = jnp.broadcast_to(place[:, None, :], (ROPE_DIM, N_HEADS, MLA_HEAD_PAD))
    wk_exp = jnp.concatenate([wk_nope.reshape(KV_LORA, -1), place.reshape(ROPE_DIM, -1),
                              jnp.zeros((CKR_PAD - KV_LORA - ROPE_DIM, N_HEADS * MLA_HEAD_PAD), F32)], axis=0)
    wv = w3[:, :, NOPE_DIM:].reshape(KV_LORA, N_HEADS * V_DIM)
    return wuq_pad, wdkv_pad, wk_exp.astype(BF16), wv.astype(BF16)


def _ctx_gqa(cache_k, cache_v):
    b, p = cache_k.shape[:2]
    k = cache_k.reshape(b, p, KV_HEADS * HEAD_DIM).astype(BF16)
    v = cache_v.reshape(b, p // KV_TILE, KV_TILE, KV_HEADS * HEAD_DIM)
    return k, jnp.swapaxes(v, 2, 3).astype(BF16)


def kernel(x_prompt, x_sample, c, cache_a_k, cache_a_v, cache_b_ckv, cache_b_krope, cache_c_k, cache_c_v, c_ctx, w_ada, b_ada, norm_g, w_mlp_in, w_mlp_out, a_w_qkv, a_sink, a_w_o, b_w_dq, b_g_q, b_w_uq, b_w_dkv, b_g_kv, b_w_ukv, b_w_o, c_w_qkv, c_g_q, c_g_k, c_w_o, g_final):
    n_batch, seq = x_prompt.shape[:2]
    n_dec, dec_seq = x_sample.shape[:2]
    cond = jnp.concatenate([c_ctx[None, :], c, jnp.zeros((8 - 1 - n_dec, D_MODEL), F32)], axis=0)
    mods_all = _ada_all(cond, w_ada, b_ada)
    tab64 = _rope_tables(dec_seq, HEAD_DIM)
    tab32 = _rope_tables(dec_seq, ROPE_DIM)
    prompt_row = lambda i: 0
    sample_row = lambda i: i + 1

    xp, xs = x_prompt, x_sample
    st = {"a_k": [], "a_v": [], "b_ckv": [], "b_kr": [], "c_k": [], "c_v": []}
    for layer in range(DEPTH):
        kind, j = layer % 3, layer // 3
        mods = mods_all[layer].reshape(8, 1, 6 * D_MODEL)
        g1, g2 = norm_g[layer, 0], norm_g[layer, 1]
        if kind == 0:
            sink = a_sink[j]
            qT, k, vT, ks, vs = _proj_gqa(xp, mods, prompt_row, g1, a_w_qkv[j], None, None, None, states=True)
            st["a_k"].append(ks)
            st["a_v"].append(vs)
            oT_p = _attn_dense(qT, k, vT, _sink_cols(sink, seq), mla=False, tq=seq)
            qT, k, vT = _proj_gqa(xs, mods, sample_row, g1, a_w_qkv[j], tab64, None, None, states=False)
            k_ctx, vT_ctx = _ctx_gqa(cache_a_k[:, j], cache_a_v[:, j])
            oT_s = _attn_window(qT, k_ctx, vT_ctx, k, vT, _sink_cols(sink, WINDOW_BLOCK), tq=512)
            wo = a_w_o[j]
        elif kind == 1:
            wuq_pad, wdkv_pad, wk_exp, wv = _mla_weights(b_w_uq[j], b_w_dkv[j], b_w_ukv[j])
            qT, ck, cs, krs = _proj_mla(xp, mods, prompt_row, g1, b_w_dq[j], b_g_q[j], wuq_pad, wdkv_pad,
                                        b_g_kv[j], None, states=True)
            st["b_ckv"].append(cs)
            st["b_kr"].append(krs)
            oT_p = _attn_dense(qT, *_expand_mla(ck, wk_exp, wv), None, mla=True, tq=seq)
            qT, ck = _proj_mla(xs, mods, sample_row, g1, b_w_dq[j], b_g_q[j], wuq_pad, wdkv_pad,
                               b_g_kv[j], tab32, states=False)
            ck_ctx = jnp.concatenate(
                [cache_b_ckv[:, j], cache_b_krope[:, j],
                 jnp.zeros(cache_b_krope.shape[:1] + cache_b_krope.shape[2:3] + (CKR_PAD - KV_LORA - ROPE_DIM,), F32)],
                axis=-1).astype(BF16)
            ck_all = jnp.concatenate([ck_ctx, ck], axis=1)
            oT_s = _attn_dense(qT, *_expand_mla(ck_all, wk_exp, wv), None, mla=True, tq=1024)
            wo = b_w_o[j]
        else:
            qT, k, vT, ks, vs = _proj_gqa(xp, mods, prompt_row, g1, c_w_qkv[j], None, c_g_q[j], c_g_k[j],
                                          states=True)
            st["c_k"].append(ks)
            st["c_v"].append(vs)
            oT_p = _attn_dense(qT, k, vT, None, mla=False, tq=seq)
            qT, k, vT = _proj_gqa(xs, mods, sample_row, g1, c_w_qkv[j], tab64, c_g_q[j], c_g_k[j], states=False)
            k_ctx, vT_ctx = _ctx_gqa(cache_c_k[:, j], cache_c_v[:, j])
            oT_s = _attn_dense(qT, jnp.concatenate([k_ctx, k], axis=1), jnp.concatenate([vT_ctx, vT], axis=1),
                               None, mla=False, tq=512)
            wo = c_w_o[j]
        gf = g_final if layer == DEPTH - 1 else None
        wo, win, wout = wo.astype(BF16), w_mlp_in[layer].astype(BF16), w_mlp_out[layer].astype(BF16)
        xp = _post(xp, oT_p, mods, prompt_row, g2, wo, win, wout, gf)
        xs = _post(xs, oT_s, mods, sample_row, g2, wo, win, wout, gf)

    def stack_heads(parts):
        return jnp.stack([p.reshape(n_batch, seq, KV_HEADS, HEAD_DIM) for p in parts], axis=1)

    return (xp, xs, stack_heads(st["a_k"]), stack_heads(st["a_v"]),
            jnp.stack(st["b_ckv"], axis=1), jnp.stack(st["b_kr"], axis=1),
            stack_heads(st["c_k"]), stack_heads(st["c_v"]))
```

```python
import functools

import jax
import jax.numpy as jnp
from jax import lax
from jax.experimental import pallas as pl
from jax.experimental.pallas import tpu as pltpu

D_MODEL = 1024
DEPTH = 4
N_HEADS = 16
KV_HEADS = 4
GROUP = N_HEADS // KV_HEADS
HEAD_DIM = 64
QKV_DIM = (N_HEADS + 2 * KV_HEADS) * HEAD_DIM
Q_LORA = 384
KV_LORA = 256
NOPE_DIM = 64
ROPE_DIM = 32
V_DIM = 64
D_FF = 4 * D_MODEL
GRID_W = 64
WINDOW_BLOCK = 128
ROPE_THETA = 10000.0
EPS = 1e-6
NEG_INF = -1e30
LOG2_E = 1.4426950408889634
ATTN_Q_SCALE = HEAD_DIM ** -0.5 * LOG2_E
MLA_Q_SCALE = (NOPE_DIM + ROPE_DIM) ** -0.5 * LOG2_E

MLA_HEAD_PAD = 128
CKR_PAD = KV_LORA + 128
ROW_TILE = 256
KV_TILE = 256
SOFTMAX_CHUNK = 256
SUBLANES = 8
TILE_GROUP = 4
SUM_ROWS = 16
FF_CHUNK = 1024
V7X_VMEM_LIMIT_BYTES = 48 * 1024 * 1024

F32 = jnp.float32
BF16 = jnp.bfloat16


def _params(n_axes):
    return pltpu.CompilerParams(dimension_semantics=("arbitrary",) * n_axes,
                                vmem_limit_bytes=V7X_VMEM_LIMIT_BYTES)


def _dot(a, b):
    return jnp.dot(a, b, preferred_element_type=F32)


def _rmsnorm(x, g):
    ms = jnp.mean(x * x, axis=-1, keepdims=True)
    return x * lax.rsqrt(ms + EPS) * g


def _norm_mod(x, g, shift, scale):
    return _rmsnorm(x, g) * (1.0 + scale) + shift


def _ada_kernel(cond_ref, w_ref, b_ref, o_ref):
    cnd = cond_ref[...]
    act = cnd / (1.0 + jnp.exp(-cnd))
    o_ref[0] = _dot(act.astype(BF16), w_ref[0].astype(BF16)) + b_ref[0]


def _ada_all(cond, w_ada, b_ada):
    tn = 1536
    n = 6 * D_MODEL
    return pl.pallas_call(
        _ada_kernel,
        grid=(DEPTH, n // tn),
        in_specs=[pl.BlockSpec((8, D_MODEL), lambda l, j: (0, 0)),
                  pl.BlockSpec((1, D_MODEL, tn), lambda l, j: (l, 0, j)),
                  pl.BlockSpec((1, 1, tn), lambda l, j: (l, 0, j))],
        out_specs=pl.BlockSpec((1, 8, tn), lambda l, j: (l, 0, j)),
        out_shape=jax.ShapeDtypeStruct((DEPTH, 8, n), F32),
        compiler_params=_params(2),
        name="ada_mod",
    )(cond, w_ada, b_ada.reshape(DEPTH, 1, n))


def _rope_t(x3, tab, half):
    a, b = x3[:, 0:half], x3[:, half:2 * half]
    c, d = x3[:, 2 * half:3 * half], x3[:, 3 * half:4 * half]
    cr, sr, cc, sc = tab[0], tab[1], tab[2], tab[3]
    return jnp.concatenate([a * cr - b * sr, a * sr + b * cr, c * cc - d * sc, c * sc + d * cc], axis=1)


def _rope_tables(t_len, dim):
    half = dim // 2
    pos = jnp.arange(t_len)
    rows = (pos // GRID_W).astype(F32)
    cols = (pos % GRID_W).astype(F32)
    freqs = ROPE_THETA ** (-jnp.arange(0, half, 2, dtype=F32) / half)
    ang_r = freqs[:, None] * rows[None, :]
    ang_c = freqs[:, None] * cols[None, :]
    return jnp.stack([jnp.cos(ang_r), jnp.sin(ang_r), jnp.cos(ang_c), jnp.sin(ang_c)])


def _proj_gqa_kernel(*refs, rope, qknorm, states):
    x_ref, mod_ref, g_ref, w_ref = refs[:4]
    pos = 4
    if rope:
        tab_ref = refs[pos]
        pos += 1
    if qknorm:
        gq_ref, gk_ref = refs[pos], refs[pos + 1]
        pos += 2
    qT_ref, k_ref, vT_ref = refs[pos:pos + 3]
    pos += 3
    tm = x_ref.shape[1]
    nq = N_HEADS * HEAD_DIM
    nk = KV_HEADS * HEAD_DIM

    mod = mod_ref[0]
    h = _norm_mod(x_ref[0], g_ref[...], mod[:, 0:D_MODEL], mod[:, D_MODEL:2 * D_MODEL])
    y = _dot(h.astype(BF16), w_ref[...])
    q3 = y[:, :nq].T.reshape(N_HEADS, HEAD_DIM, tm)
    k3 = y[:, nq:nq + nk].T.reshape(KV_HEADS, HEAD_DIM, tm)
    v = y[:, nq + nk:]
    if qknorm:
        q3 = q3 * lax.rsqrt(jnp.mean(q3 * q3, axis=1, keepdims=True) + EPS) * gq_ref[...]
        k3 = k3 * lax.rsqrt(jnp.mean(k3 * k3, axis=1, keepdims=True) + EPS) * gk_ref[...]
    if states:
        ks_ref, vs_ref = refs[pos], refs[pos + 1]
        ks_ref[0] = k3.reshape(nk, tm).T if qknorm else y[:, nq:nq + nk]
        vs_ref[0] = v
    if rope:
        tab = tab_ref[...]
        q3 = _rope_t(q3, tab, HEAD_DIM // 4)
        k3 = _rope_t(k3, tab, HEAD_DIM // 4)
    qT_ref[0] = (q3 * ATTN_Q_SCALE).reshape(nq, tm).astype(BF16)
    k_ref[0] = k3.reshape(nk, tm).T.astype(BF16)
    vT_ref[0, 0] = v.T.astype(BF16)


def _proj_gqa(x, mods, mod_row, g, w, tab, gq, gk, *, states):
    b, t, _ = x.shape
    tm = ROW_TILE
    nq = N_HEADS * HEAD_DIM
    nk = KV_HEADS * HEAD_DIM
    rope = tab is not None
    qknorm = gq is not None
    in_specs = [pl.BlockSpec((1, tm, D_MODEL), lambda i, j: (i, j, 0)),
                pl.BlockSpec((1, 1, 6 * D_MODEL), lambda i, j: (mod_row(i), 0, 0)),
                pl.BlockSpec((1, D_MODEL), lambda i, j: (0, 0)),
                pl.BlockSpec((D_MODEL, QKV_DIM), lambda i, j: (0, 0))]
    args = [x, mods, g.reshape(1, D_MODEL), w.astype(BF16)]
    if rope:
        in_specs.append(pl.BlockSpec((4, HEAD_DIM // 4, tm), lambda i, j: (0, 0, j)))
        args.append(tab)
    if qknorm:
        in_specs += [pl.BlockSpec((HEAD_DIM, 1), lambda i, j: (0, 0))] * 2
        args += [gq.reshape(HEAD_DIM, 1), gk.reshape(HEAD_DIM, 1)]
    out_specs = [pl.BlockSpec((1, nq, tm), lambda i, j: (i, 0, j)),
                 pl.BlockSpec((1, tm, nk), lambda i, j: (i, j, 0)),
                 pl.BlockSpec((1, 1, nk, tm), lambda i, j: (i, j, 0, 0))]
    out_shape = [jax.ShapeDtypeStruct((b, nq, t), BF16),
                 jax.ShapeDtypeStruct((b, t, nk), BF16),
                 jax.ShapeDtypeStruct((b, t // tm, nk, tm), BF16)]
    if states:
        out_specs += [pl.BlockSpec((1, tm, nk), lambda i, j: (i, j, 0))] * 2
        out_shape += [jax.ShapeDtypeStruct((b, t, nk), F32)] * 2
    return pl.pallas_call(
        functools.partial(_proj_gqa_kernel, rope=rope, qknorm=qknorm, states=states),
        grid=(b, t // tm), in_specs=in_specs, out_specs=out_specs, out_shape=out_shape,
        compiler_params=_params(2), name="proj_gqa",
    )(*args)


def _proj_mla_kernel(*refs, rope, states):
    x_ref, mod_ref, g_ref, wdq_ref, gq_ref, wuq_ref, wdkv_ref, gkv_ref = refs[:8]
    pos = 8
    if rope:
        tab_ref = refs[pos]
        pos += 1
    qT_ref, ck_ref = refs[pos], refs[pos + 1]
    pos += 2
    tm = x_ref.shape[1]
    quarter = ROPE_DIM // 4

    mod = mod_ref[0]
    h = _norm_mod(x_ref[0], g_ref[...], mod[:, 0:D_MODEL], mod[:, D_MODEL:2 * D_MODEL]).astype(BF16)
    qn = _rmsnorm(_dot(h, wdq_ref[...]), gq_ref[...])
    q3 = _dot(qn.astype(BF16), wuq_ref[...]).T.reshape(N_HEADS, MLA_HEAD_PAD, tm)
    ckv = _dot(h, wdkv_ref[...])
    cn = _rmsnorm(ckv[:, :KV_LORA], gkv_ref[...])
    kr = ckv[:, KV_LORA:]
    if states:
        cs_ref, krs_ref = refs[pos], refs[pos + 1]
        cs_ref[0] = cn
        krs_ref[0] = kr[:, :ROPE_DIM]
    if rope:
        tab = tab_ref[...]
        q_rot = _rope_t(q3[:, NOPE_DIM:NOPE_DIM + ROPE_DIM], tab, quarter)
        q3 = jnp.concatenate([q3[:, :NOPE_DIM], q_rot, q3[:, NOPE_DIM + ROPE_DIM:]], axis=1)
        krT = kr.T
        kr_rot = _rope_t(krT[:ROPE_DIM].reshape(1, ROPE_DIM, tm), tab, quarter).reshape(ROPE_DIM, tm)
        kr = jnp.concatenate([kr_rot, krT[ROPE_DIM:]], axis=0).T
    qT_ref[0] = (q3 * MLA_Q_SCALE).reshape(N_HEADS * MLA_HEAD_PAD, tm).astype(BF16)
    ck_ref[0] = jnp.concatenate([cn, kr], axis=1).astype(BF16)


def _proj_mla(x, mods, mod_row, g, wdq, gq, wuq_pad, wdkv_pad, gkv, tab, *, states):
    b, t, _ = x.shape
    tm = ROW_TILE
    rope = tab is not None
    nqp = N_HEADS * MLA_HEAD_PAD
    const = lambda i, j: (0, 0)
    in_specs = [pl.BlockSpec((1, tm, D_MODEL), lambda i, j: (i, j, 0)),
                pl.BlockSpec((1, 1, 6 * D_MODEL), lambda i, j: (mod_row(i), 0, 0)),
                pl.BlockSpec((1, D_MODEL), const),
                pl.BlockSpec((D_MODEL, Q_LORA), const),
                pl.BlockSpec((1, Q_LORA), const),
                pl.BlockSpec((Q_LORA, nqp), const),
                pl.BlockSpec((D_MODEL, CKR_PAD), const),
                pl.BlockSpec((1, KV_LORA), const)]
    args = [x, mods, g.reshape(1, D_MODEL), wdq.astype(BF16), gq.reshape(1, Q_LORA), wuq_pad,
            wdkv_pad, gkv.reshape(1, KV_LORA)]
    if rope:
        in_specs.append(pl.BlockSpec((4, ROPE_DIM // 4, tm), lambda i, j: (0, 0, j)))
        args.append(tab)
    out_specs = [pl.BlockSpec((1, nqp, tm), lambda i, j: (i, 0, j)),
                 pl.BlockSpec((1, tm, CKR_PAD), lambda i, j: (i, j, 0))]
    out_shape = [jax.ShapeDtypeStruct((b, nqp, t), BF16),
                 jax.ShapeDtypeStruct((b, t, CKR_PAD), BF16)]
    if states:
        out_specs += [pl.BlockSpec((1, tm, KV_LORA), lambda i, j: (i, j, 0)),
                      pl.BlockSpec((1, tm, ROPE_DIM), lambda i, j: (i, j, 0))]
        out_shape += [jax.ShapeDtypeStruct((b, t, KV_LORA), F32),
                      jax.ShapeDtypeStruct((b, t, ROPE_DIM), F32)]
    return pl.pallas_call(
        functools.partial(_proj_mla_kernel, rope=rope, states=states),
        grid=(b, t // tm), in_specs=in_specs, out_specs=out_specs, out_shape=out_shape,
        compiler_params=_params(2), name="proj_mla",
    )(*args)


def _expand_kernel(ck_ref, wk_ref, wv_ref, k_ref, vT_ref):
    ck = ck_ref[0]
    k_ref[0] = _dot(ck, wk_ref[...]).astype(BF16)
    vT_ref[0, 0] = _dot(ck[:, :KV_LORA], wv_ref[...]).T.astype(BF16)


def _expand_mla(ck, wk_exp, wv):
    b, t, _ = ck.shape
    tm = ROW_TILE
    nkp = N_HEADS * MLA_HEAD_PAD
    nv = N_HEADS * V_DIM
    return pl.pallas_call(
        _expand_kernel,
        grid=(b, t // tm),
        in_specs=[pl.BlockSpec((1, tm, CKR_PAD), lambda i, j: (i, j, 0)),
                  pl.BlockSpec((CKR_PAD, nkp), lambda i, j: (0, 0)),
                  pl.BlockSpec((KV_LORA, nv), lambda i, j: (0, 0))],
        out_specs=[pl.BlockSpec((1, tm, nkp), lambda i, j: (i, j, 0)),
                   pl.BlockSpec((1, 1, nv, tm), lambda i, j: (i, j, 0, 0))],
        out_shape=[jax.ShapeDtypeStruct((b, t, nkp), BF16),
                   jax.ShapeDtypeStruct((b, t // tm, nv, tm), BF16)],
        compiler_params=_params(2), name="expand_mla",
    )(ck, wk_exp, wv)


def _q_operand(q_ref, u, lo, width, group):
    qg = jnp.concatenate(
        [q_ref[0, (u * group + g) * HEAD_DIM:(u * group + g + 1) * HEAD_DIM, lo:lo + width] for g in range(group)],
        axis=1)
    z = jnp.zeros_like(qg)
    return jnp.concatenate([qg, z] if u == 0 else [z, qg], axis=0)


def _attn_dense_kernel(*refs, mla, tq, n_blocks, has_sink):
    q_ref, k_ref, v_ref = refs[:3]
    pos = 3
    sink_ref = None
    if has_sink:
        sink_ref = refs[pos]
        pos += 1
    o_ref = refs[pos]
    pos += 1
    nbuf = TILE_GROUP
    s_refs = [refs[pos + u * nbuf:pos + (u + 1) * nbuf] for u in range(2)]
    pos += 2 * nbuf
    p_refs = [refs[pos + u * nbuf:pos + (u + 1) * nbuf] for u in range(2)]
    pos += 2 * nbuf
    if mla:
        acc_ref, m_ref = refs[pos:]
        group = 1
    else:
        qp_ref, acc_ref, m_ref = refs[pos:]
        group = GROUP
        for u in range(2):
            qp_ref[u] = _q_operand(q_ref, u, 0, tq, group)
    ng = group * tq
    tk = KV_TILE
    cw = SOFTMAX_CHUNK
    ones_rows = jnp.ones((SUM_ROWS, tk), BF16)
    sub = SUBLANES
    v_tiles = V_DIM // sub

    for u in range(2):
        m_ref[u] = jnp.broadcast_to(sink_ref[u], (sub, ng)) if has_sink else jnp.full((sub, ng), NEG_INF, F32)
        acc_ref[u, :v_tiles] = jnp.zeros((v_tiles, sub, ng), F32)
        acc_ref[u, v_tiles:] = jnp.full((SUM_ROWS // sub, sub, ng), 1.0 if has_sink else 0.0, F32)

    def scores(j, par):
        off = j * tk if isinstance(j, int) else pl.multiple_of(j * tk, tk)
        for u in range(2):
            if mla:
                kp = k_ref[0, pl.ds(off, tk), u * MLA_HEAD_PAD:(u + 1) * MLA_HEAD_PAD]
                qop = q_ref[0, u * MLA_HEAD_PAD:(u + 1) * MLA_HEAD_PAD, :]
            else:
                kp = k_ref[0, pl.ds(off, tk), :]
                qop = qp_ref[u]
            s_refs[u][par][...] = _dot(kp, qop).reshape(tk // sub, sub, ng)

    def accumulate(j, par):
        for u in range(2):
            s_ref, p_ref = s_refs[u][par], p_refs[u][par]
            m_blk = jnp.max(s_ref[...], axis=0)
            for shift in (4, 2, 1):
                m_blk = jnp.maximum(m_blk, pltpu.roll(m_blk, shift, 0))
            m_prev = m_ref[u]
            m_new = jnp.maximum(m_prev, m_blk)
            m_ref[u] = m_new
            alpha = jnp.exp2(m_prev - m_new)
            for c in range(ng // cw):
                cols = slice(c * cw, (c + 1) * cw)
                p = jnp.exp2(s_ref[:, :, cols] - m_ref[u, :, cols])
                p_ref[:, cols] = p.reshape(tk, cw).astype(BF16)
            v_ones = jnp.concatenate([v_ref[0, j, u * V_DIM:(u + 1) * V_DIM, :], ones_rows], axis=0)
            pv = _dot(v_ones, p_ref[...]).reshape(v_tiles + SUM_ROWS // sub, sub, ng)
            acc_ref[u] = alpha * acc_ref[u] + pv

    def tile_group(t0, n):
        for r in range(n):
            scores(t0 + r, r)
        for r in range(n):
            accumulate(t0 + r, r)

    head = n_blocks % nbuf
    if head:
        tile_group(0, head)
    n_loop = n_blocks // nbuf
    if n_loop == 1:
        tile_group(head, nbuf)
    elif n_loop > 1:
        def body(i, carry):
            tile_group(head + nbuf * i, nbuf)
            return carry
        lax.fori_loop(0, n_loop, body, 0)

    for u in range(2):
        o = (acc_ref[u, :v_tiles] / acc_ref[u, v_tiles]).reshape(V_DIM, ng)
        if mla:
            o_ref[0, u * V_DIM:(u + 1) * V_DIM, :] = o.astype(BF16)
        else:
            for g in range(group):
                r0 = (u * group + g) * HEAD_DIM
                o_ref[0, r0:r0 + HEAD_DIM, :] = o[:, g * tq:(g + 1) * tq].astype(BF16)


def _attn_dense(qT, k, vT, sink_cols, *, mla, tq):
    b, _, t = qT.shape
    tk = KV_TILE
    n = k.shape[1]
    n_pairs = N_HEADS // 2 if mla else KV_HEADS // 2
    q_rows = 2 * MLA_HEAD_PAD if mla else 2 * GROUP * HEAD_DIM
    o_rows = 2 * V_DIM if mla else 2 * GROUP * HEAD_DIM
    k_cols = 2 * MLA_HEAD_PAD if mla else 2 * HEAD_DIM
    ng = tq if mla else GROUP * tq
    in_specs = [pl.BlockSpec((1, q_rows, tq), lambda i, p, j: (i, p, j)),
                pl.BlockSpec((1, n, k_cols), lambda i, p, j: (i, 0, p)),
                pl.BlockSpec((1, n // tk, 2 * V_DIM, tk), lambda i, p, j: (i, 0, p, 0))]
    args = [qT, k, vT]
    if sink_cols is not None:
        in_specs.append(pl.BlockSpec((2, 1, ng), lambda i, p, j: (p, 0, 0)))
        args.append(sink_cols)
    sub = SUBLANES
    scratch = [pltpu.VMEM((tk // sub, sub, ng), F32)] * (2 * TILE_GROUP)
    scratch += [pltpu.VMEM((tk, ng), BF16)] * (2 * TILE_GROUP)
    if not mla:
        scratch.append(pltpu.VMEM((2, 2 * HEAD_DIM, ng), BF16))
    scratch += [pltpu.VMEM((2, (V_DIM + SUM_ROWS) // sub, sub, ng), F32),
                pltpu.VMEM((2, sub, ng), F32)]
    return pl.pallas_call(
        functools.partial(_attn_dense_kernel, mla=mla, tq=tq, n_blocks=n // tk,
                          has_sink=sink_cols is not None),
        grid=(b, n_pairs, t // tq), in_specs=in_specs,
        out_specs=pl.BlockSpec((1, o_rows, tq), lambda i, p, j: (i, p, j)),
        out_shape=jax.ShapeDtypeStruct((b, N_HEADS * V_DIM, t), BF16),
        scratch_shapes=scratch, compiler_params=_params(3),
        name="attn_mla" if mla else "attn_gqa",
    )(*args)


def _attn_window_kernel(q_ref, kc_ref, vc_ref, kp_ref, vp_ref, kcur_ref, vcur_ref, kn_ref, vn_ref,
                        sink_ref, o_ref, *scratch, tq, n_steps):
    wb = WINDOW_BLOCK
    step = pl.program_id(2)
    n_sub = tq // wb
    s_refs, p_refs = scratch[:2 * n_sub], scratch[2 * n_sub:]
    ng = GROUP * wb
    k_off = lax.broadcasted_iota(jnp.int32, (wb, ng), 0)
    q_off = lax.broadcasted_iota(jnp.int32, (wb, ng), 1) % wb
    keep_prev = k_off >= q_off
    keep_next = k_off <= q_off
    first = step == 0
    last = step == n_steps - 1
    n_ctx = kc_ref.shape[1]
    ones_rows = jnp.ones((SUM_ROWS, n_ctx + 3 * wb), BF16)

    def kv_block(i):
        if i < 0:
            return kp_ref[0], vp_ref[0, 0]
        if i >= n_sub:
            return kn_ref[0], vn_ref[0, 0]
        c, r = divmod(i * wb, KV_TILE)
        return kcur_ref[0, i * wb:(i + 1) * wb, :], vcur_ref[0, c, :, r:r + wb]

    sub = SUBLANES
    n_keys = n_ctx + 3 * wb
    v_tiles = V_DIM // sub

    def tiles(x):
        return x.reshape(x.shape[0] // sub, sub, x.shape[1])

    for i in range(n_sub):
        kcat = jnp.concatenate([kc_ref[0], kv_block(i - 1)[0], kv_block(i)[0], kv_block(i + 1)[0]], axis=0)
        keep0 = jnp.logical_and(keep_prev, jnp.logical_not(first)) if i == 0 else keep_prev
        keep2 = jnp.logical_and(keep_next, jnp.logical_not(last)) if i == n_sub - 1 else keep_next
        for u in range(2):
            s = _dot(kcat, _q_operand(q_ref, u, i * wb, wb, GROUP))
            s_ref = s_refs[2 * i + u]
            t0, t1, t2 = n_ctx // sub, (n_ctx + wb) // sub, (n_ctx + 2 * wb) // sub
            s_ref[:t0] = tiles(s[:n_ctx])
            s_ref[t0:t1] = tiles(jnp.where(keep0, s[n_ctx:n_ctx + wb], NEG_INF))
            s_ref[t1:t2] = tiles(s[n_ctx + wb:n_ctx + 2 * wb])
            s_ref[t2:] = tiles(jnp.where(keep2, s[n_ctx + 2 * wb:], NEG_INF))

    for i in range(n_sub):
        vcat = jnp.concatenate([vc_ref[0, 0], kv_block(i - 1)[1], kv_block(i)[1], kv_block(i + 1)[1]], axis=1)
        for u in range(2):
            s_ref, p_ref = s_refs[2 * i + u], p_refs[2 * i + u]
            sink = jnp.broadcast_to(sink_ref[u], (sub, ng))
            m = jnp.max(s_ref[...], axis=0)
            for shift in (4, 2, 1):
                m = jnp.maximum(m, pltpu.roll(m, shift, 0))
            m = jnp.maximum(m, sink)
            p_ref[...] = jnp.exp2(s_ref[...] - m).reshape(n_keys, ng).astype(BF16)
            v_ones = jnp.concatenate([vcat[u * V_DIM:(u + 1) * V_DIM, :], ones_rows], axis=0)
            r = tiles(_dot(v_ones, p_ref[...]))
            o = (r[:v_tiles] / (r[v_tiles] + jnp.exp2(sink - m))).reshape(V_DIM, ng)
            for g in range(GROUP):
                r0 = (u * GROUP + g) * HEAD_DIM
                o_ref[0, r0:r0 + HEAD_DIM, i * wb:(i + 1) * wb] = o[:, g * wb:(g + 1) * wb].astype(BF16)


def _attn_window(qT, k_ctx, vT_ctx, k, vT, sink_cols, *, tq):
    b, _, t = qT.shape
    wb = WINDOW_BLOCK
    n_ctx = k_ctx.shape[1]
    n_steps = t // tq
    n_sub = tq // wb
    n_wb = t // wb
    per_tile = KV_TILE // wb
    pair_rows = 2 * GROUP * HEAD_DIM
    ng = GROUP * wb

    def prev_blk(j):
        return jnp.maximum(j * n_sub - 1, 0)

    def next_blk(j):
        return jnp.minimum((j + 1) * n_sub, n_wb - 1)

    in_specs = [
        pl.BlockSpec((1, pair_rows, tq), lambda i, p, j: (i, p, j)),
        pl.BlockSpec((1, n_ctx, 2 * HEAD_DIM), lambda i, p, j: (i, 0, p)),
        pl.BlockSpec((1, n_ctx // KV_TILE, 2 * V_DIM, KV_TILE), lambda i, p, j: (i, 0, p, 0)),
        pl.BlockSpec((1, wb, 2 * HEAD_DIM), lambda i, p, j: (i, prev_blk(j), p)),
        pl.BlockSpec((1, 1, 2 * V_DIM, wb), lambda i, p, j: (i, prev_blk(j) // per_tile, p, prev_blk(j) % per_tile)),
        pl.BlockSpec((1, tq, 2 * HEAD_DIM), lambda i, p, j: (i, j, p)),
        pl.BlockSpec((1, tq // KV_TILE, 2 * V_DIM, KV_TILE), lambda i, p, j: (i, j, p, 0)),
        pl.BlockSpec((1, wb, 2 * HEAD_DIM), lambda i, p, j: (i, next_blk(j), p)),
        pl.BlockSpec((1, 1, 2 * V_DIM, wb), lambda i, p, j: (i, next_blk(j) // per_tile, p, next_blk(j) % per_tile)),
        pl.BlockSpec((2, 1, ng), lambda i, p, j: (p, 0, 0)),
    ]
    return pl.pallas_call(
        functools.partial(_attn_window_kernel, tq=tq, n_steps=n_steps),
        grid=(b, KV_HEADS // 2, n_steps), in_specs=in_specs,
        out_specs=pl.BlockSpec((1, pair_rows, tq), lambda i, p, j: (i, p, j)),
        out_shape=jax.ShapeDtypeStruct((b, N_HEADS * HEAD_DIM, t), BF16),
        scratch_shapes=([pltpu.VMEM(((n_ctx + 3 * wb) // SUBLANES, SUBLANES, ng), F32)] * (2 * n_sub)
                        + [pltpu.VMEM((n_ctx + 3 * wb, ng), BF16)] * (2 * n_sub)),
        compiler_params=_params(3), name="attn_window",
    )(qT, k_ctx, vT_ctx, k, vT, k, vT, k, vT, sink_cols)


def _post_kernel(*refs, final):
    x_ref, oT_ref, mod_ref, g2_ref, wo_ref, win_ref, wout_ref = refs[:7]
    pos = 7
    if final:
        gf_ref = refs[pos]
        pos += 1
    out_ref = refs[pos]
    mod = mod_ref[0]
    gate, shift2, scale2, gate2 = (mod[:, c * D_MODEL:(c + 1) * D_MODEL] for c in range(2, 6))
    attn = lax.dot_general(oT_ref[0], wo_ref[...], (((0,), (0,)), ((), ())), preferred_element_type=F32)
    x1 = x_ref[0] + gate * attn
    h2 = _norm_mod(x1, g2_ref[...], shift2, scale2).astype(BF16)
    acc = jnp.zeros_like(x1)
    for c in range(D_FF // FF_CHUNK):
        u = jnp.maximum(_dot(h2, win_ref[:, c * FF_CHUNK:(c + 1) * FF_CHUNK]), 0.0)
        acc = acc + _dot((u * u).astype(BF16), wout_ref[c * FF_CHUNK:(c + 1) * FF_CHUNK, :])
    x2 = x1 + gate2 * acc
    out_ref[0] = _rmsnorm(x2, gf_ref[...]) if final else x2


def _post(x, oT, mods, mod_row, g2, wo, win, wout, g_final):
    b, t, _ = x.shape
    tm = ROW_TILE
    final = g_final is not None
    const = lambda i, j: (0, 0)
    in_specs = [pl.BlockSpec((1, tm, D_MODEL), lambda i, j: (i, j, 0)),
                pl.BlockSpec((1, D_MODEL, tm), lambda i, j: (i, 0, j)),
                pl.BlockSpec((1, 1, 6 * D_MODEL), lambda i, j: (mod_row(i), 0, 0)),
                pl.BlockSpec((1, D_MODEL), const),
                pl.BlockSpec((D_MODEL, D_MODEL), const, pipeline_mode=pl.Buffered(1)),
                pl.BlockSpec((D_MODEL, D_FF), const, pipeline_mode=pl.Buffered(1)),
                pl.BlockSpec((D_FF, D_MODEL), const, pipeline_mode=pl.Buffered(1))]
    args = [x, oT, mods, g2.reshape(1, D_MODEL), wo, win, wout]
    if final:
        in_specs.append(pl.BlockSpec((1, D_MODEL), const))
        args.append(g_final.reshape(1, D_MODEL))
    return pl.pallas_call(
        functools.partial(_post_kernel, final=final),
        grid=(b, t // tm), in_specs=in_specs,
        out_specs=pl.BlockSpec((1, tm, D_MODEL), lambda i, j: (i, j, 0)),
        out_shape=jax.ShapeDtypeStruct((b, t, D_MODEL), F32),
        compiler_params=_params(2), name="post_mlp",
    )(*args)


def _sink_cols(sink, width):
    return jnp.repeat((sink.astype(F32) * LOG2_E).reshape(KV_HEADS, 1, GROUP), width, axis=2)


def _mla_weights(w_uq, w_dkv, w_ukv):
    wuq_pad = jnp.pad(w_uq.reshape(Q_LORA, N_HEADS, NOPE_DIM + ROPE_DIM),
                      ((0, 0), (0, 0), (0, MLA_HEAD_PAD - NOPE_DIM - ROPE_DIM)))
    wuq_pad = wuq_pad.reshape(Q_LORA, N_HEADS * MLA_HEAD_PAD).astype(BF16)
    wdkv_pad = jnp.pad(w_dkv, ((0, 0), (0, CKR_PAD - KV_LORA - ROPE_DIM))).astype(BF16)
    w3 = w_ukv.reshape(KV_LORA, N_HEADS, NOPE_DIM + V_DIM)
    wk_nope = jnp.pad(w3[:, :, :NOPE_DIM], ((0, 0), (0, 0), (0, MLA_HEAD_PAD - NOPE_DIM)))
    place = jnp.pad(jnp.eye(ROPE_DIM, dtype=F32), ((0, 0), (NOPE_DIM, MLA_HEAD_PAD - NOPE_DIM - ROPE_DIM)))
    place = jnp.broadcast_to(place[:, None, :], (ROPE_DIM, N_HEADS, MLA_HEAD_PAD))
    wk_exp = jnp.concatenate([wk_nope.reshape(KV_LORA, -1), place.reshape(ROPE_DIM, -1),
                              jnp.zeros((CKR_PAD - KV_LORA - ROPE_DIM, N_HEADS * MLA_HEAD_PAD), F32)], axis=0)
    wv = w3[:, :, NOPE_DIM:].reshape(KV_LORA, N_HEADS * V_DIM)
    return wuq_pad, wdkv_pad, wk_exp.astype(BF16), wv.astype(BF16)


def _ctx_gqa(cache_k, cache_v):
    b, p = cache_k.shape[:2]
    k = cache_k.reshape(b, p, KV_HEADS * HEAD_DIM).astype(BF16)
    v = cache_v.reshape(b, p // KV_TILE, KV_TILE, KV_HEADS * HEAD_DIM)
    return k, jnp.swapaxes(v, 2, 3).astype(BF16)


def kernel(x_prompt, x_sample, c, cache_a_k, cache_a_v, cache_b_ckv, cache_b_krope, cache_c_k, cache_c_v, c_ctx, w_ada, b_ada, norm_g, w_mlp_in, w_mlp_out, a_w_qkv, a_sink, a_w_o, b_w_dq, b_g_q, b_w_uq, b_w_dkv, b_g_kv, b_w_ukv, b_w_o, c_w_qkv, c_g_q, c_g_k, c_w_o, g_final):
    n_batch, seq = x_prompt.shape[:2]
    n_dec, dec_seq = x_sample.shape[:2]
    cond = jnp.concatenate([c_ctx[None, :], c, jnp.zeros((8 - 1 - n_dec, D_MODEL), F32)], axis=0)
    mods_all = _ada_all(cond, w_ada, b_ada)
    tab64 = _rope_tables(dec_seq, HEAD_DIM)
    tab32 = _rope_tables(dec_seq, ROPE_DIM)
    prompt_row = lambda i: 0
    sample_row = lambda i: i + 1

    xp, xs = x_prompt, x_sample
    st = {"a_k": [], "a_v": [], "b_ckv": [], "b_kr": [], "c_k": [], "c_v": []}
    for layer in range(DEPTH):
        kind, j = layer % 3, layer // 3
        mods = mods_all[layer].reshape(8, 1, 6 * D_MODEL)
        g1, g2 = norm_g[layer, 0], norm_g[layer, 1]
        if kind == 0:
            sink = a_sink[j]
            qT, k, vT, ks, vs = _proj_gqa(xp, mods, prompt_row, g1, a_w_qkv[j], None, None, None, states=True)
            st["a_k"].append(ks)
            st["a_v"].append(vs)
            oT_p = _attn_dense(qT, k, vT, _sink_cols(sink, seq), mla=False, tq=seq)
            qT, k, vT = _proj_gqa(xs, mods, sample_row, g1, a_w_qkv[j], tab64, None, None, states=False)
            k_ctx, vT_ctx = _ctx_gqa(cache_a_k[:, j], cache_a_v[:, j])
            oT_s = _attn_window(qT, k_ctx, vT_ctx, k, vT, _sink_cols(sink, WINDOW_BLOCK), tq=512)
            wo = a_w_o[j]
        elif kind == 1:
            wuq_pad, wdkv_pad, wk_exp, wv = _mla_weights(b_w_uq[j], b_w_dkv[j], b_w_ukv[j])
            qT, ck, cs, krs = _proj_mla(xp, mods, prompt_row, g1, b_w_dq[j], b_g_q[j], wuq_pad, wdkv_pad,
                                        b_g_kv[j], None, states=True)
            st["b_ckv"].append(cs)
            st["b_kr"].append(krs)
            oT_p = _attn_dense(qT, *_expand_mla(ck, wk_exp, wv), None, mla=True, tq=seq)
            qT, ck = _proj_mla(xs, mods, sample_row, g1, b_w_dq[j], b_g_q[j], wuq_pad, wdkv_pad,
                               b_g_kv[j], tab32, states=False)
            ck_ctx = jnp.concatenate(
                [cache_b_ckv[:, j], cache_b_krope[:, j],
                 jnp.zeros(cache_b_krope.shape[:1] + cache_b_krope.shape[2:3] + (CKR_PAD - KV_LORA - ROPE_DIM,), F32)],
                axis=-1).astype(BF16)
            ck_all = jnp.concatenate([ck_ctx, ck], axis=1)
            oT_s = _attn_dense(qT, *_expand_mla(ck_all, wk_exp, wv), None, mla=True, tq=1024)
            wo = b_w_o[j]
        else:
            qT, k, vT, ks, vs = _proj_gqa(xp, mods, prompt_row, g1, c_w_qkv[j], None, c_g_q[j], c_g_k[j],
                                          states=True)
            st["c_k"].append(ks)
            st["c_v"].append(vs)
            oT_p = _attn_dense(qT, k, vT, None, mla=False, tq=seq)
            qT, k, vT = _proj_gqa(xs, mods, sample_row, g1, c_w_qkv[j], tab64, c_g_q[j], c_g_k[j], states=False)
            k_ctx, vT_ctx = _ctx_gqa(cache_c_k[:, j], cache_c_v[:, j])
            oT_s = _attn_dense(qT, jnp.concatenate([k_ctx, k], axis=1), jnp.concatenate([vT_ctx, vT], axis=1),
                               None, mla=False, tq=512)
            wo = c_w_o[j]
        gf = g_final if layer == DEPTH - 1 else None
        wo, win, wout = wo.astype(BF16), w_mlp_in[layer].astype(BF16), w_mlp_out[layer].astype(BF16)
        xp = _post(xp, oT_p, mods, prompt_row, g2, wo, win, wout, gf)
        xs = _post(xs, oT_s, mods, sample_row, g2, wo, win, wout, gf)

    def stack_heads(parts):
        return jnp.stack([p.reshape(n_batch, seq, KV_HEADS, HEAD_DIM) for p in parts], axis=1)

    return (xp, xs, stack_heads(st["a_k"]), stack_heads(st["a_v"]),
            jnp.stack(st["b_ckv"], axis=1), jnp.stack(st["b_kr"], axis=1),
            stack_heads(st["c_k"]), stack_heads(st["c_v"]))
```

```python
import functools

import jax
import jax.numpy as jnp
from jax import lax
from jax.experimental import pallas as pl
from jax.experimental.pallas import tpu as pltpu

D_MODEL = 1024
DEPTH = 4
N_HEADS = 16
KV_HEADS = 4
GROUP = N_HEADS // KV_HEADS
HEAD_DIM = 64
QKV_DIM = (N_HEADS + 2 * KV_HEADS) * HEAD_DIM
Q_LORA = 384
KV_LORA = 256
NOPE_DIM = 64
ROPE_DIM = 32
V_DIM = 64
D_FF = 4 * D_MODEL
GRID_W = 64
WINDOW_BLOCK = 128
ROPE_THETA = 10000.0
EPS = 1e-6
NEG_INF = -1e30
LOG2_E = 1.4426950408889634
ATTN_Q_SCALE = HEAD_DIM ** -0.5 * LOG2_E
MLA_Q_SCALE = (NOPE_DIM + ROPE_DIM) ** -0.5 * LOG2_E

MLA_HEAD_PAD = 128
CKR_PAD = KV_LORA + 128
ROW_TILE = 256
MLP_ROW_TILE = 512
KV_TILE = 256
SOFTMAX_CHUNK = 256
SUBLANES = 8
TILE_GROUP = 4
SUM_ROWS = 16
FF_CHUNK = 1024
V7X_VMEM_LIMIT_BYTES = 48 * 1024 * 1024

F32 = jnp.float32
BF16 = jnp.bfloat16


def _params(n_axes):
    return pltpu.CompilerParams(dimension_semantics=("arbitrary",) * n_axes,
                                vmem_limit_bytes=V7X_VMEM_LIMIT_BYTES)


def _dot(a, b):
    return jnp.dot(a, b, preferred_element_type=F32)


def _rmsnorm(x, g):
    ms = jnp.mean(x * x, axis=-1, keepdims=True)
    return x * lax.rsqrt(ms + EPS) * g


def _norm_mod(x, g, shift, scale):
    return _rmsnorm(x, g) * (1.0 + scale) + shift


def _ada_kernel(cond_ref, w_ref, b_ref, o_ref):
    cnd = cond_ref[...]
    act = cnd / (1.0 + jnp.exp(-cnd))
    o_ref[0] = _dot(act.astype(BF16), w_ref[0].astype(BF16)) + b_ref[0]


def _ada_all(cond, w_ada, b_ada):
    tn = 1536
    n = 6 * D_MODEL
    return pl.pallas_call(
        _ada_kernel,
        grid=(DEPTH, n // tn),
        in_specs=[pl.BlockSpec((8, D_MODEL), lambda l, j: (0, 0)),
                  pl.BlockSpec((1, D_MODEL, tn), lambda l, j: (l, 0, j)),
                  pl.BlockSpec((1, 1, tn), lambda l, j: (l, 0, j))],
        out_specs=pl.BlockSpec((1, 8, tn), lambda l, j: (l, 0, j)),
        out_shape=jax.ShapeDtypeStruct((DEPTH, 8, n), F32),
        compiler_params=_params(2),
        name="ada_mod",
    )(cond, w_ada, b_ada.reshape(DEPTH, 1, n))


def _rope_t(x3, tab, half):
    a, b = x3[:, 0:half], x3[:, half:2 * half]
    c, d = x3[:, 2 * half:3 * half], x3[:, 3 * half:4 * half]
    cr, sr, cc, sc = tab[0], tab[1], tab[2], tab[3]
    return jnp.concatenate([a * cr - b * sr, a * sr + b * cr, c * cc - d * sc, c * sc + d * cc], axis=1)


def _rope_tables(t_len, dim):
    half = dim // 2
    pos = jnp.arange(t_len)
    rows = (pos // GRID_W).astype(F32)
    cols = (pos % GRID_W).astype(F32)
    freqs = ROPE_THETA ** (-jnp.arange(0, half, 2, dtype=F32) / half)
    ang_r = freqs[:, None] * rows[None, :]
    ang_c = freqs[:, None] * cols[None, :]
    return jnp.stack([jnp.cos(ang_r), jnp.sin(ang_r), jnp.cos(ang_c), jnp.sin(ang_c)])


def _proj_gqa_kernel(*refs, rope, qknorm, states):
    x_ref, mod_ref, g_ref, w_ref = refs[:4]
    pos = 4
    if rope:
        tab_ref = refs[pos]
        pos += 1
    if qknorm:
        gq_ref, gk_ref = refs[pos], refs[pos + 1]
        pos += 2
    qT_ref, k_ref, vT_ref = refs[pos:pos + 3]
    pos += 3
    tm = x_ref.shape[1]
    nq = N_HEADS * HEAD_DIM
    nk = KV_HEADS * HEAD_DIM

    mod = mod_ref[0]
    h = _norm_mod(x_ref[0], g_ref[...], mod[:, 0:D_MODEL], mod[:, D_MODEL:2 * D_MODEL])
    y = _dot(h.astype(BF16), w_ref[...])
    q3 = y[:, :nq].T.reshape(N_HEADS, HEAD_DIM, tm)
    k3 = y[:, nq:nq + nk].T.reshape(KV_HEADS, HEAD_DIM, tm)
    v = y[:, nq + nk:]
    if qknorm:
        q3 = q3 * lax.rsqrt(jnp.mean(q3 * q3, axis=1, keepdims=True) + EPS) * gq_ref[...]
        k3 = k3 * lax.rsqrt(jnp.mean(k3 * k3, axis=1, keepdims=True) + EPS) * gk_ref[...]
    if states:
        ks_ref, vs_ref = refs[pos], refs[pos + 1]
        ks_ref[0] = k3.reshape(nk, tm).T if qknorm else y[:, nq:nq + nk]
        vs_ref[0] = v
    if rope:
        tab = tab_ref[...]
        q3 = _rope_t(q3, tab, HEAD_DIM // 4)
        k3 = _rope_t(k3, tab, HEAD_DIM // 4)
    qT_ref[0] = (q3 * ATTN_Q_SCALE).reshape(nq, tm).astype(BF16)
    k_ref[0] = k3.reshape(nk, tm).T.astype(BF16)
    vT_ref[0, 0] = v.T.astype(BF16)


def _proj_gqa(x, mods, mod_row, g, w, tab, gq, gk, *, states):
    b, t, _ = x.shape
    tm = ROW_TILE
    nq = N_HEADS * HEAD_DIM
    nk = KV_HEADS * HEAD_DIM
    rope = tab is not None
    qknorm = gq is not None
    in_specs = [pl.BlockSpec((1, tm, D_MODEL), lambda i, j: (i, j, 0)),
                pl.BlockSpec((1, 1, 6 * D_MODEL), lambda i, j: (mod_row(i), 0, 0)),
                pl.BlockSpec((1, D_MODEL), lambda i, j: (0, 0)),
                pl.BlockSpec((D_MODEL, QKV_DIM), lambda i, j: (0, 0))]
    args = [x, mods, g.reshape(1, D_MODEL), w.astype(BF16)]
    if rope:
        in_specs.append(pl.BlockSpec((4, HEAD_DIM // 4, tm), lambda i, j: (0, 0, j)))
        args.append(tab)
    if qknorm:
        in_specs += [pl.BlockSpec((HEAD_DIM, 1), lambda i, j: (0, 0))] * 2
        args += [gq.reshape(HEAD_DIM, 1), gk.reshape(HEAD_DIM, 1)]
    out_specs = [pl.BlockSpec((1, nq, tm), lambda i, j: (i, 0, j)),
                 pl.BlockSpec((1, tm, nk), lambda i, j: (i, j, 0)),
                 pl.BlockSpec((1, 1, nk, tm), lambda i, j: (i, j, 0, 0))]
    out_shape = [jax.ShapeDtypeStruct((b, nq, t), BF16),
                 jax.ShapeDtypeStruct((b, t, nk), BF16),
                 jax.ShapeDtypeStruct((b, t // tm, nk, tm), BF16)]
    if states:
        out_specs += [pl.BlockSpec((1, tm, nk), lambda i, j: (i, j, 0))] * 2
        out_shape += [jax.ShapeDtypeStruct((b, t, nk), F32)] * 2
    return pl.pallas_call(
        functools.partial(_proj_gqa_kernel, rope=rope, qknorm=qknorm, states=states),
        grid=(b, t // tm), in_specs=in_specs, out_specs=out_specs, out_shape=out_shape,
        compiler_params=_params(2), name="proj_gqa",
    )(*args)


def _proj_mla_kernel(*refs, rope, states):
    x_ref, mod_ref, g_ref, wdq_ref, gq_ref, wuq_ref, wdkv_ref, gkv_ref = refs[:8]
    pos = 8
    if rope:
        tab_ref = refs[pos]
        pos += 1
    qT_ref, ck_ref = refs[pos], refs[pos + 1]
    pos += 2
    tm = x_ref.shape[1]
    quarter = ROPE_DIM // 4

    mod = mod_ref[0]
    h = _norm_mod(x_ref[0], g_ref[...], mod[:, 0:D_MODEL], mod[:, D_MODEL:2 * D_MODEL]).astype(BF16)
    qn = _rmsnorm(_dot(h, wdq_ref[...]), gq_ref[...])
    q3 = _dot(qn.astype(BF16), wuq_ref[...]).T.reshape(N_HEADS, MLA_HEAD_PAD, tm)
    ckv = _dot(h, wdkv_ref[...])
    cn = _rmsnorm(ckv[:, :KV_LORA], gkv_ref[...])
    kr = ckv[:, KV_LORA:]
    if states:
        cs_ref, krs_ref = refs[pos], refs[pos + 1]
        cs_ref[0] = cn
        krs_ref[0] = kr[:, :ROPE_DIM]
    if rope:
        tab = tab_ref[...]
        q_rot = _rope_t(q3[:, NOPE_DIM:NOPE_DIM + ROPE_DIM], tab, quarter)
        q3 = jnp.concatenate([q3[:, :NOPE_DIM], q_rot, q3[:, NOPE_DIM + ROPE_DIM:]], axis=1)
        krT = kr.T
        kr_rot = _rope_t(krT[:ROPE_DIM].reshape(1, ROPE_DIM, tm), tab, quarter).reshape(ROPE_DIM, tm)
        kr = jnp.concatenate([kr_rot, krT[ROPE_DIM:]], axis=0).T
    qT_ref[0] = (q3 * MLA_Q_SCALE).reshape(N_HEADS * MLA_HEAD_PAD, tm).astype(BF16)
    ck_ref[0] = jnp.concatenate([cn, kr], axis=1).astype(BF16)


def _proj_mla(x, mods, mod_row, g, wdq, gq, wuq_pad, wdkv_pad, gkv, tab, *, states):
    b, t, _ = x.shape
    tm = ROW_TILE
    rope = tab is not None
    nqp = N_HEADS * MLA_HEAD_PAD
    const = lambda i, j: (0, 0)
    in_specs = [pl.BlockSpec((1, tm, D_MODEL), lambda i, j: (i, j, 0)),
                pl.BlockSpec((1, 1, 6 * D_MODEL), lambda i, j: (mod_row(i), 0, 0)),
                pl.BlockSpec((1, D_MODEL), const),
                pl.BlockSpec((D_MODEL, Q_LORA), const),
                pl.BlockSpec((1, Q_LORA), const),
                pl.BlockSpec((Q_LORA, nqp), const),
                pl.BlockSpec((D_MODEL, CKR_PAD), const),
                pl.BlockSpec((1, KV_LORA), const)]
    args = [x, mods, g.reshape(1, D_MODEL), wdq.astype(BF16), gq.reshape(1, Q_LORA), wuq_pad,
            wdkv_pad, gkv.reshape(1, KV_LORA)]
    if rope:
        in_specs.append(pl.BlockSpec((4, ROPE_DIM // 4, tm), lambda i, j: (0, 0, j)))
        args.append(tab)
    out_specs = [pl.BlockSpec((1, nqp, tm), lambda i, j: (i, 0, j)),
                 pl.BlockSpec((1, tm, CKR_PAD), lambda i, j: (i, j, 0))]
    out_shape = [jax.ShapeDtypeStruct((b, nqp, t), BF16),
                 jax.ShapeDtypeStruct((b, t, CKR_PAD), BF16)]
    if states:
        out_specs += [pl.BlockSpec((1, tm, KV_LORA), lambda i, j: (i, j, 0)),
                      pl.BlockSpec((1, tm, ROPE_DIM), lambda i, j: (i, j, 0))]
        out_shape += [jax.ShapeDtypeStruct((b, t, KV_LORA), F32),
                      jax.ShapeDtypeStruct((b, t, ROPE_DIM), F32)]
    return pl.pallas_call(
        functools.partial(_proj_mla_kernel, rope=rope, states=states),
        grid=(b, t // tm), in_specs=in_specs, out_specs=out_specs, out_shape=out_shape,
        compiler_params=_params(2), name="proj_mla",
    )(*args)


def _expand_kernel(ck_ref, wk_ref, wv_ref, k_ref, vT_ref):
    ck = ck_ref[0]
    k_ref[0] = _dot(ck, wk_ref[...]).astype(BF16)
    vT_ref[0, 0] = _dot(ck[:, :KV_LORA], wv_ref[...]).T.astype(BF16)


def _expand_mla(ck, wk_exp, wv):
    b, t, _ = ck.shape
    tm = ROW_TILE
    nkp = N_HEADS * MLA_HEAD_PAD
    nv = N_HEADS * V_DIM
    return pl.pallas_call(
        _expand_kernel,
        grid=(b, t // tm),
        in_specs=[pl.BlockSpec((1, tm, CKR_PAD), lambda i, j: (i, j, 0)),
                  pl.BlockSpec((CKR_PAD, nkp), lambda i, j: (0, 0)),
                  pl.BlockSpec((KV_LORA, nv), lambda i, j: (0, 0))],
        out_specs=[pl.BlockSpec((1, tm, nkp), lambda i, j: (i, j, 0)),
                   pl.BlockSpec((1, 1, nv, tm), lambda i, j: (i, j, 0, 0))],
        out_shape=[jax.ShapeDtypeStruct((b, t, nkp), BF16),
                   jax.ShapeDtypeStruct((b, t // tm, nv, tm), BF16)],
        compiler_params=_params(2), name="expand_mla",
    )(ck, wk_exp, wv)


def _q_operand(q_ref, u, lo, width, group):
    qg = jnp.concatenate(
        [q_ref[0, (u * group + g) * HEAD_DIM:(u * group + g + 1) * HEAD_DIM, lo:lo + width] for g in range(group)],
        axis=1)
    z = jnp.zeros_like(qg)
    return jnp.concatenate([qg, z] if u % 2 == 0 else [z, qg], axis=0)


def _attn_dense_kernel(*refs, mla, units, tq, n_blocks, has_sink):
    q_ref, k_ref, v_ref = refs[:3]
    pos = 3
    sink_ref = None
    if has_sink:
        sink_ref = refs[pos]
        pos += 1
    o_ref = refs[pos]
    pos += 1
    nbuf = min(TILE_GROUP, n_blocks)
    s_refs = [refs[pos + u * nbuf:pos + (u + 1) * nbuf] for u in range(units)]
    pos += units * nbuf
    p_refs = [refs[pos + u * nbuf:pos + (u + 1) * nbuf] for u in range(units)]
    pos += units * nbuf
    if mla:
        acc_ref, m_ref = refs[pos:]
        group = 1
    else:
        qp_ref, acc_ref, m_ref = refs[pos:]
        group = GROUP
        for u in range(units):
            qp_ref[u] = _q_operand(q_ref, u, 0, tq, group)
    ng = group * tq
    tk = KV_TILE
    cw = SOFTMAX_CHUNK
    ones_rows = jnp.ones((SUM_ROWS, tk), BF16)
    sub = SUBLANES
    v_tiles = V_DIM // sub

    for u in range(units):
        m_ref[u] = jnp.broadcast_to(sink_ref[u], (sub, ng)) if has_sink else jnp.full((sub, ng), NEG_INF, F32)
        acc_ref[u, :v_tiles] = jnp.zeros((v_tiles, sub, ng), F32)
        acc_ref[u, v_tiles:] = jnp.full((SUM_ROWS // sub, sub, ng), 1.0 if has_sink else 0.0, F32)

    def scores(j, par):
        off = j * tk if isinstance(j, int) else pl.multiple_of(j * tk, tk)
        for u in range(units):
            if mla:
                kp = k_ref[0, pl.ds(off, tk), u * MLA_HEAD_PAD:(u + 1) * MLA_HEAD_PAD]
                qop = q_ref[0, u * MLA_HEAD_PAD:(u + 1) * MLA_HEAD_PAD, :]
            else:
                kp = k_ref[0, pl.ds(off, tk), (u // 2) * 2 * HEAD_DIM:(u // 2 + 1) * 2 * HEAD_DIM]
                qop = qp_ref[u]
            s_refs[u][par][...] = _dot(kp, qop).reshape(tk // sub, sub, ng)

    def accumulate(j, par):
        for u in range(units):
            s_ref, p_ref = s_refs[u][par], p_refs[u][par]
            m_blk = jnp.max(s_ref[...], axis=0)
            for shift in (4, 2, 1):
                m_blk = jnp.maximum(m_blk, pltpu.roll(m_blk, shift, 0))
            m_prev = m_ref[u]
            m_new = jnp.maximum(m_prev, m_blk)
            m_ref[u] = m_new
            alpha = jnp.exp2(m_prev - m_new)
            for c in range(ng // cw):
                cols = slice(c * cw, (c + 1) * cw)
                p = jnp.exp2(s_ref[:, :, cols] - m_ref[u, :, cols])
                p_ref[:, cols] = p.reshape(tk, cw).astype(BF16)
            v_ones = jnp.concatenate([v_ref[0, j, u * V_DIM:(u + 1) * V_DIM, :], ones_rows], axis=0)
            pv = _dot(v_ones, p_ref[...]).reshape(v_tiles + SUM_ROWS // sub, sub, ng)
            acc_ref[u] = alpha * acc_ref[u] + pv

    def tile_group(t0, n):
        for r in range(n):
            scores(t0 + r, r)
        for r in range(n):
            accumulate(t0 + r, r)

    head = n_blocks % nbuf
    if head:
        tile_group(0, head)
    n_loop = n_blocks // nbuf
    if n_loop == 1:
        tile_group(head, nbuf)
    elif n_loop > 1:
        def body(i, carry):
            tile_group(head + nbuf * i, nbuf)
            return carry
        lax.fori_loop(0, n_loop, body, 0)

    for u in range(units):
        o = (acc_ref[u, :v_tiles] / acc_ref[u, v_tiles]).reshape(V_DIM, ng)
        if mla:
            o_ref[0, u * V_DIM:(u + 1) * V_DIM, :] = o.astype(BF16)
        else:
            for g in range(group):
                r0 = (u * group + g) * HEAD_DIM
                o_ref[0, r0:r0 + HEAD_DIM, :] = o[:, g * tq:(g + 1) * tq].astype(BF16)


def _attn_dense(qT, k, vT, sink_cols, *, mla, tq, pairs=1):
    b, _, t = qT.shape
    tk = KV_TILE
    n = k.shape[1]
    n_pairs = N_HEADS // 2 if mla else KV_HEADS // 2
    units = 2 * pairs
    q_rows = pairs * (2 * MLA_HEAD_PAD if mla else 2 * GROUP * HEAD_DIM)
    o_rows = pairs * (2 * V_DIM if mla else 2 * GROUP * HEAD_DIM)
    k_cols = pairs * (2 * MLA_HEAD_PAD if mla else 2 * HEAD_DIM)
    ng = tq if mla else GROUP * tq
    nbuf = min(TILE_GROUP, n // tk)
    in_specs = [pl.BlockSpec((1, q_rows, tq), lambda i, p, j: (i, p, j)),
                pl.BlockSpec((1, n, k_cols), lambda i, p, j: (i, 0, p)),
                pl.BlockSpec((1, n // tk, units * V_DIM, tk), lambda i, p, j: (i, 0, p, 0))]
    args = [qT, k, vT]
    if sink_cols is not None:
        in_specs.append(pl.BlockSpec((units, 1, ng), lambda i, p, j: (p, 0, 0)))
        args.append(sink_cols)
    sub = SUBLANES
    scratch = [pltpu.VMEM((tk // sub, sub, ng), F32)] * (units * nbuf)
    scratch += [pltpu.VMEM((tk, ng), BF16)] * (units * nbuf)
    if not mla:
        scratch.append(pltpu.VMEM((units, 2 * HEAD_DIM, ng), BF16))
    scratch += [pltpu.VMEM((units, (V_DIM + SUM_ROWS) // sub, sub, ng), F32),
                pltpu.VMEM((units, sub, ng), F32)]
    return pl.pallas_call(
        functools.partial(_attn_dense_kernel, mla=mla, units=units, tq=tq, n_blocks=n // tk,
                          has_sink=sink_cols is not None),
        grid=(b, n_pairs // pairs, t // tq), in_specs=in_specs,
        out_specs=pl.BlockSpec((1, o_rows, tq), lambda i, p, j: (i, p, j)),
        out_shape=jax.ShapeDtypeStruct((b, N_HEADS * V_DIM, t), BF16),
        scratch_shapes=scratch, compiler_params=_params(3),
        name="attn_mla" if mla else "attn_gqa",
    )(*args)


def _attn_window_kernel(q_ref, kc_ref, vc_ref, kp_ref, vp_ref, kcur_ref, vcur_ref, kn_ref, vn_ref,
                        sink_ref, o_ref, *scratch, tq, n_steps):
    wb = WINDOW_BLOCK
    step = pl.program_id(2)
    n_sub = tq // wb
    s_refs, p_refs = scratch[:2 * n_sub], scratch[2 * n_sub:]
    ng = GROUP * wb
    k_off = lax.broadcasted_iota(jnp.int32, (wb, ng), 0)
    q_off = lax.broadcasted_iota(jnp.int32, (wb, ng), 1) % wb
    keep_prev = k_off >= q_off
    keep_next = k_off <= q_off
    first = step == 0
    last = step == n_steps - 1
    n_ctx = kc_ref.shape[1]
    ones_rows = jnp.ones((SUM_ROWS, n_ctx + 3 * wb), BF16)

    def kv_block(i):
        if i < 0:
            return kp_ref[0], vp_ref[0, 0]
        if i >= n_sub:
            return kn_ref[0], vn_ref[0, 0]
        c, r = divmod(i * wb, KV_TILE)
        return kcur_ref[0, i * wb:(i + 1) * wb, :], vcur_ref[0, c, :, r:r + wb]

    sub = SUBLANES
    n_keys = n_ctx + 3 * wb
    v_tiles = V_DIM // sub

    def tiles(x):
        return x.reshape(x.shape[0] // sub, sub, x.shape[1])

    for i in range(n_sub):
        kcat = jnp.concatenate([kc_ref[0], kv_block(i - 1)[0], kv_block(i)[0], kv_block(i + 1)[0]], axis=0)
        keep0 = jnp.logical_and(keep_prev, jnp.logical_not(first)) if i == 0 else keep_prev
        keep2 = jnp.logical_and(keep_next, jnp.logical_not(last)) if i == n_sub - 1 else keep_next
        for u in range(2):
            s = _dot(kcat, _q_operand(q_ref, u, i * wb, wb, GROUP))
            s_ref = s_refs[2 * i + u]
            t0, t1, t2 = n_ctx // sub, (n_ctx + wb) // sub, (n_ctx + 2 * wb) // sub
            s_ref[:t0] = tiles(s[:n_ctx])
            s_ref[t0:t1] = tiles(jnp.where(keep0, s[n_ctx:n_ctx + wb], NEG_INF))
            s_ref[t1:t2] = tiles(s[n_ctx + wb:n_ctx + 2 * wb])
            s_ref[t2:] = tiles(jnp.where(keep2, s[n_ctx + 2 * wb:], NEG_INF))

    for i in range(n_sub):
        vcat = jnp.concatenate([vc_ref[0, 0], kv_block(i - 1)[1], kv_block(i)[1], kv_block(i + 1)[1]], axis=1)
        for u in range(2):
            s_ref, p_ref = s_refs[2 * i + u], p_refs[2 * i + u]
            sink = jnp.broadcast_to(sink_ref[u], (sub, ng))
            m = jnp.max(s_ref[...], axis=0)
            for shift in (4, 2, 1):
                m = jnp.maximum(m, pltpu.roll(m, shift, 0))
            m = jnp.maximum(m, sink)
            p_ref[...] = jnp.exp2(s_ref[...] - m).reshape(n_keys, ng).astype(BF16)
            v_ones = jnp.concatenate([vcat[u * V_DIM:(u + 1) * V_DIM, :], ones_rows], axis=0)
            r = tiles(_dot(v_ones, p_ref[...]))
            o = (r[:v_tiles] / (r[v_tiles] + jnp.exp2(sink - m))).reshape(V_DIM, ng)
            for g in range(GROUP):
                r0 = (u * GROUP + g) * HEAD_DIM
                o_ref[0, r0:r0 + HEAD_DIM, i * wb:(i + 1) * wb] = o[:, g * wb:(g + 1) * wb].astype(BF16)


def _attn_window(qT, k_ctx, vT_ctx, k, vT, sink_cols, *, tq):
    b, _, t = qT.shape
    wb = WINDOW_BLOCK
    n_ctx = k_ctx.shape[1]
    n_steps = t // tq
    n_sub = tq // wb
    n_wb = t // wb
    per_tile = KV_TILE // wb
    pair_rows = 2 * GROUP * HEAD_DIM
    ng = GROUP * wb

    def prev_blk(j):
        return jnp.maximum(j * n_sub - 1, 0)

    def next_blk(j):
        return jnp.minimum((j + 1) * n_sub, n_wb - 1)

    in_specs = [
        pl.BlockSpec((1, pair_rows, tq), lambda i, p, j: (i, p, j)),
        pl.BlockSpec((1, n_ctx, 2 * HEAD_DIM), lambda i, p, j: (i, 0, p)),
        pl.BlockSpec((1, n_ctx // KV_TILE, 2 * V_DIM, KV_TILE), lambda i, p, j: (i, 0, p, 0)),
        pl.BlockSpec((1, wb, 2 * HEAD_DIM), lambda i, p, j: (i, prev_blk(j), p)),
        pl.BlockSpec((1, 1, 2 * V_DIM, wb), lambda i, p, j: (i, prev_blk(j) // per_tile, p, prev_blk(j) % per_tile)),
        pl.BlockSpec((1, tq, 2 * HEAD_DIM), lambda i, p, j: (i, j, p)),
        pl.BlockSpec((1, tq // KV_TILE, 2 * V_DIM, KV_TILE), lambda i, p, j: (i, j, p, 0)),
        pl.BlockSpec((1, wb, 2 * HEAD_DIM), lambda i, p, j: (i, next_blk(j), p)),
        pl.BlockSpec((1, 1, 2 * V_DIM, wb), lambda i, p, j: (i, next_blk(j) // per_tile, p, next_blk(j) % per_tile)),
        pl.BlockSpec((2, 1, ng), lambda i, p, j: (p, 0, 0)),
    ]
    return pl.pallas_call(
        functools.partial(_attn_window_kernel, tq=tq, n_steps=n_steps),
        grid=(b, KV_HEADS // 2, n_steps), in_specs=in_specs,
        out_specs=pl.BlockSpec((1, pair_rows, tq), lambda i, p, j: (i, p, j)),
        out_shape=jax.ShapeDtypeStruct((b, N_HEADS * HEAD_DIM, t), BF16),
        scratch_shapes=([pltpu.VMEM(((n_ctx + 3 * wb) // SUBLANES, SUBLANES, ng), F32)] * (2 * n_sub)
                        + [pltpu.VMEM((n_ctx + 3 * wb, ng), BF16)] * (2 * n_sub)),
        compiler_params=_params(3), name="attn_window",
    )(qT, k_ctx, vT_ctx, k, vT, k, vT, k, vT, sink_cols)


def _post_kernel(*refs, final):
    x_ref, oT_ref, mod_ref, g2_ref, wo_ref, win_ref, wout_ref = refs[:7]
    pos = 7
    if final:
        gf_ref = refs[pos]
        pos += 1
    out_ref = refs[pos]
    mod = mod_ref[0]
    gate, shift2, scale2, gate2 = (mod[:, c * D_MODEL:(c + 1) * D_MODEL] for c in range(2, 6))
    attn = lax.dot_general(oT_ref[0], wo_ref[...], (((0,), (0,)), ((), ())), preferred_element_type=F32)
    x1 = x_ref[0] + gate * attn
    h2 = _norm_mod(x1, g2_ref[...], shift2, scale2).astype(BF16)
    acc = jnp.zeros_like(x1)
    for c in range(D_FF // FF_CHUNK):
        u = jnp.maximum(_dot(h2, win_ref[:, c * FF_CHUNK:(c + 1) * FF_CHUNK]), 0.0)
        acc = acc + _dot((u * u).astype(BF16), wout_ref[c * FF_CHUNK:(c + 1) * FF_CHUNK, :])
    x2 = x1 + gate2 * acc
    out_ref[0] = _rmsnorm(x2, gf_ref[...]) if final else x2


def _post(x, oT, mods, mod_row, g2, wo, win, wout, g_final):
    b, t, _ = x.shape
    tm = min(t, MLP_ROW_TILE)
    final = g_final is not None
    const = lambda i, j: (0, 0)
    in_specs = [pl.BlockSpec((1, tm, D_MODEL), lambda i, j: (i, j, 0)),
                pl.BlockSpec((1, D_MODEL, tm), lambda i, j: (i, 0, j)),
                pl.BlockSpec((1, 1, 6 * D_MODEL), lambda i, j: (mod_row(i), 0, 0)),
                pl.BlockSpec((1, D_MODEL), const),
                pl.BlockSpec((D_MODEL, D_MODEL), const, pipeline_mode=pl.Buffered(1)),
                pl.BlockSpec((D_MODEL, D_FF), const, pipeline_mode=pl.Buffered(1)),
                pl.BlockSpec((D_FF, D_MODEL), const, pipeline_mode=pl.Buffered(1))]
    args = [x, oT, mods, g2.reshape(1, D_MODEL), wo, win, wout]
    if final:
        in_specs.append(pl.BlockSpec((1, D_MODEL), const))
        args.append(g_final.reshape(1, D_MODEL))
    return pl.pallas_call(
        functools.partial(_post_kernel, final=final),
        grid=(b, t // tm), in_specs=in_specs,
        out_specs=pl.BlockSpec((1, tm, D_MODEL), lambda i, j: (i, j, 0)),
        out_shape=jax.ShapeDtypeStruct((b, t, D_MODEL), F32),
        compiler_params=_params(2), name="post_mlp",
    )(*args)


def _sink_cols(sink, width):
    return jnp.repeat((sink.astype(F32) * LOG2_E).reshape(KV_HEADS, 1, GROUP), width, axis=2)


def _mla_weights(w_uq, w_dkv, w_ukv):
    wuq_pad = jnp.pad(w_uq.reshape(Q_LORA, N_HEADS, NOPE_DIM + ROPE_DIM),
                      ((0, 0), (0, 0), (0, MLA_HEAD_PAD - NOPE_DIM - ROPE_DIM)))
    wuq_pad = wuq_pad.reshape(Q_LORA, N_HEADS * MLA_HEAD_PAD).astype(BF16)
    wdkv_pad = jnp.pad(w_dkv, ((0, 0), (0, CKR_PAD - KV_LORA - ROPE_DIM))).astype(BF16)
    w3 = w_ukv.reshape(KV_LORA, N_HEADS, NOPE_DIM + V_DIM)
    wk_nope = jnp.pad(w3[:, :, :NOPE_DIM], ((0, 0), (0, 0), (0, MLA_HEAD_PAD - NOPE_DIM)))
    place = jnp.pad(jnp.eye(ROPE_DIM, dtype=F32), ((0, 0), (NOPE_DIM, MLA_HEAD_PAD - NOPE_DIM - ROPE_DIM)))
    place = jnp.broadcast_to(place[:, None, :], (ROPE_DIM, N_HEADS, MLA_HEAD_PAD))
    wk_exp = jnp.concatenate([wk_nope.reshape(KV_LORA, -1), place.reshape(ROPE_DIM, -1),
                              jnp.zeros((CKR_PAD - KV_LORA - ROPE_DIM, N_HEADS * MLA_HEAD_PAD), F32)], axis=0)
    wv = w3[:, :, NOPE_DIM:].reshape(KV_LORA, N_HEADS * V_DIM)
    return wuq_pad, wdkv_pad, wk_exp.astype(BF16), wv.astype(BF16)


def _ctx_gqa(cache_k, cache_v):
    b, p = cache_k.shape[:2]
    k = cache_k.reshape(b, p, KV_HEADS * HEAD_DIM).astype(BF16)
    v = cache_v.reshape(b, p // KV_TILE, KV_TILE, KV_HEADS * HEAD_DIM)
    return k, jnp.swapaxes(v, 2, 3).astype(BF16)


def kernel(x_prompt, x_sample, c, cache_a_k, cache_a_v, cache_b_ckv, cache_b_krope, cache_c_k, cache_c_v, c_ctx, w_ada, b_ada, norm_g, w_mlp_in, w_mlp_out, a_w_qkv, a_sink, a_w_o, b_w_dq, b_g_q, b_w_uq, b_w_dkv, b_g_kv, b_w_ukv, b_w_o, c_w_qkv, c_g_q, c_g_k, c_w_o, g_final):
    n_batch, seq = x_prompt.shape[:2]
    n_dec, dec_seq = x_sample.shape[:2]
    cond = jnp.concatenate([c_ctx[None, :], c, jnp.zeros((8 - 1 - n_dec, D_MODEL), F32)], axis=0)
    mods_all = _ada_all(cond, w_ada, b_ada)
    tab64 = _rope_tables(dec_seq, HEAD_DIM)
    tab32 = _rope_tables(dec_seq, ROPE_DIM)
    prompt_row = lambda i: 0
    sample_row = lambda i: i + 1

    xp, xs = x_prompt, x_sample
    st = {"a_k": [], "a_v": [], "b_ckv": [], "b_kr": [], "c_k": [], "c_v": []}
    for layer in range(DEPTH):
        kind, j = layer % 3, layer // 3
        mods = mods_all[layer].reshape(8, 1, 6 * D_MODEL)
        g1, g2 = norm_g[layer, 0], norm_g[layer, 1]
        if kind == 0:
            sink = a_sink[j]
            qT, k, vT, ks, vs = _proj_gqa(xp, mods, prompt_row, g1, a_w_qkv[j], None, None, None, states=True)
            st["a_k"].append(ks)
            st["a_v"].append(vs)
            oT_p = _attn_dense(qT, k, vT, _sink_cols(sink, seq), mla=False, tq=seq, pairs=KV_HEADS // 2)
            qT, k, vT = _proj_gqa(xs, mods, sample_row, g1, a_w_qkv[j], tab64, None, None, states=False)
            k_ctx, vT_ctx = _ctx_gqa(cache_a_k[:, j], cache_a_v[:, j])
            oT_s = _attn_window(qT, k_ctx, vT_ctx, k, vT, _sink_cols(sink, WINDOW_BLOCK), tq=512)
            wo = a_w_o[j]
        elif kind == 1:
            wuq_pad, wdkv_pad, wk_exp, wv = _mla_weights(b_w_uq[j], b_w_dkv[j], b_w_ukv[j])
            qT, ck, cs, krs = _proj_mla(xp, mods, prompt_row, g1, b_w_dq[j], b_g_q[j], wuq_pad, wdkv_pad,
                                        b_g_kv[j], None, states=True)
            st["b_ckv"].append(cs)
            st["b_kr"].append(krs)
            oT_p = _attn_dense(qT, *_expand_mla(ck, wk_exp, wv), None, mla=True, tq=seq, pairs=N_HEADS // 4)
            qT, ck = _proj_mla(xs, mods, sample_row, g1, b_w_dq[j], b_g_q[j], wuq_pad, wdkv_pad,
                               b_g_kv[j], tab32, states=False)
            ck_ctx = jnp.concatenate(
                [cache_b_ckv[:, j], cache_b_krope[:, j],
                 jnp.zeros(cache_b_krope.shape[:1] + cache_b_krope.shape[2:3] + (CKR_PAD - KV_LORA - ROPE_DIM,), F32)],
                axis=-1).astype(BF16)
            ck_all = jnp.concatenate([ck_ctx, ck], axis=1)
            oT_s = _attn_dense(qT, *_expand_mla(ck_all, wk_exp, wv), None, mla=True, tq=1024)
            wo = b_w_o[j]
        else:
            qT, k, vT, ks, vs = _proj_gqa(xp, mods, prompt_row, g1, c_w_qkv[j], None, c_g_q[j], c_g_k[j],
                                          states=True)
            st["c_k"].append(ks)
            st["c_v"].append(vs)
            oT_p = _attn_dense(qT, k, vT, None, mla=False, tq=seq, pairs=KV_HEADS // 2)
            qT, k, vT = _proj_gqa(xs, mods, sample_row, g1, c_w_qkv[j], tab64, c_g_q[j], c_g_k[j], states=False)
            k_ctx, vT_ctx = _ctx_gqa(cache_c_k[:, j], cache_c_v[:, j])
            oT_s = _attn_dense(qT, jnp.concatenate([k_ctx, k], axis=1), jnp.concatenate([vT_ctx, vT], axis=1),
                               None, mla=False, tq=512)
            wo = c_w_o[j]
        gf = g_final if layer == DEPTH - 1 else None
        wo, win, wout = wo.astype(BF16), w_mlp_in[layer].astype(BF16), w_mlp_out[layer].astype(BF16)
        xp = _post(xp, oT_p, mods, prompt_row, g2, wo, win, wout, gf)
        xs = _post(xs, oT_s, mods, sample_row, g2, wo, win, wout, gf)

    def stack_heads(parts):
        return jnp.stack([p.reshape(n_batch, seq, KV_HEADS, HEAD_DIM) for p in parts], axis=1)

    return (xp, xs, stack_heads(st["a_k"]), stack_heads(st["a_v"]),
            jnp.stack(st["b_ckv"], axis=1), jnp.stack(st["b_kr"], axis=1),
            stack_heads(st["c_k"]), stack_heads(st["c_v"]))
```

```python
import functools

import jax
import jax.numpy as jnp
from jax import lax
from jax.experimental import pallas as pl
from jax.experimental.pallas import tpu as pltpu

D_MODEL = 1024
DEPTH = 4
N_HEADS = 16
KV_HEADS = 4
GROUP = N_HEADS // KV_HEADS
HEAD_DIM = 64
QKV_DIM = (N_HEADS + 2 * KV_HEADS) * HEAD_DIM
Q_LORA = 384
KV_LORA = 256
NOPE_DIM = 64
ROPE_DIM = 32
V_DIM = 64
D_FF = 4 * D_MODEL
GRID_W = 64
WINDOW_BLOCK = 128
ROPE_THETA = 10000.0
EPS = 1e-6
NEG_INF = -1e30
LOG2_E = 1.4426950408889634
ATTN_Q_SCALE = HEAD_DIM ** -0.5 * LOG2_E
MLA_Q_SCALE = (NOPE_DIM + ROPE_DIM) ** -0.5 * LOG2_E

MLA_HEAD_PAD = 128
CKR_PAD = KV_LORA + 128
ROW_TILE = 256
PROJ_STEP_ROWS = 512
MLP_ROW_TILE = 512
KV_TILE = 256
SOFTMAX_CHUNK = 256
SUBLANES = 8
TILE_BLOCKS = 1
TILE_GROUP = 4
SUM_ROWS = 16
FF_CHUNK = 1024
V7X_VMEM_LIMIT_BYTES = 48 * 1024 * 1024

F32 = jnp.float32
BF16 = jnp.bfloat16


def _params(n_axes):
    return pltpu.CompilerParams(dimension_semantics=("arbitrary",) * n_axes,
                                vmem_limit_bytes=V7X_VMEM_LIMIT_BYTES)


def _dot(a, b):
    return jnp.dot(a, b, preferred_element_type=F32)


def _rmsnorm(x, g):
    ms = jnp.mean(x * x, axis=-1, keepdims=True)
    return x * lax.rsqrt(ms + EPS) * g


def _norm_mod(x, g, shift, scale):
    return _rmsnorm(x, g) * (1.0 + scale) + shift


def _ada_kernel(cond_ref, w_ref, b_ref, o_ref):
    cnd = cond_ref[...]
    act = cnd / (1.0 + jnp.exp(-cnd))
    o_ref[0] = _dot(act.astype(BF16), w_ref[0].astype(BF16)) + b_ref[0]


def _ada_all(cond, w_ada, b_ada):
    tn = 1536
    n = 6 * D_MODEL
    return pl.pallas_call(
        _ada_kernel,
        grid=(DEPTH, n // tn),
        in_specs=[pl.BlockSpec((8, D_MODEL), lambda l, j: (0, 0)),
                  pl.BlockSpec((1, D_MODEL, tn), lambda l, j: (l, 0, j)),
                  pl.BlockSpec((1, 1, tn), lambda l, j: (l, 0, j))],
        out_specs=pl.BlockSpec((1, 8, tn), lambda l, j: (l, 0, j)),
        out_shape=jax.ShapeDtypeStruct((DEPTH, 8, n), F32),
        compiler_params=_params(2),
        name="ada_mod",
    )(cond, w_ada, b_ada.reshape(DEPTH, 1, n))


def _rope_t(x3, tab, half):
    a, b = x3[:, 0:half], x3[:, half:2 * half]
    c, d = x3[:, 2 * half:3 * half], x3[:, 3 * half:4 * half]
    cr, sr, cc, sc = tab[0], tab[1], tab[2], tab[3]
    return jnp.concatenate([a * cr - b * sr, a * sr + b * cr, c * cc - d * sc, c * sc + d * cc], axis=1)


def _rope_tables(t_len, dim):
    half = dim // 2
    pos = jnp.arange(t_len)
    rows = (pos // GRID_W).astype(F32)
    cols = (pos % GRID_W).astype(F32)
    freqs = ROPE_THETA ** (-jnp.arange(0, half, 2, dtype=F32) / half)
    ang_r = freqs[:, None] * rows[None, :]
    ang_c = freqs[:, None] * cols[None, :]
    return jnp.stack([jnp.cos(ang_r), jnp.sin(ang_r), jnp.cos(ang_c), jnp.sin(ang_c)])


def _proj_gqa_kernel(*refs, rope, qknorm, states):
    x_ref, mod_ref, g_ref, w_ref = refs[:4]
    pos = 4
    if rope:
        tab_ref = refs[pos]
        pos += 1
    if qknorm:
        gq_ref, gk_ref = refs[pos], refs[pos + 1]
        pos += 2
    qT_ref, k_ref, vT_ref = refs[pos:pos + 3]
    pos += 3
    tm = ROW_TILE
    nq = N_HEADS * HEAD_DIM
    nk = KV_HEADS * HEAD_DIM

    mod = mod_ref[0]
    for sub_tile in range(x_ref.shape[1] // tm):
        rows = slice(sub_tile * tm, (sub_tile + 1) * tm)
        h = _norm_mod(x_ref[0, rows], g_ref[...], mod[:, 0:D_MODEL], mod[:, D_MODEL:2 * D_MODEL])
        y = _dot(h.astype(BF16), w_ref[...])
        q3 = y[:, :nq].T.reshape(N_HEADS, HEAD_DIM, tm)
        k3 = y[:, nq:nq + nk].T.reshape(KV_HEADS, HEAD_DIM, tm)
        v = y[:, nq + nk:]
        if qknorm:
            q3 = q3 * lax.rsqrt(jnp.mean(q3 * q3, axis=1, keepdims=True) + EPS) * gq_ref[...]
            k3 = k3 * lax.rsqrt(jnp.mean(k3 * k3, axis=1, keepdims=True) + EPS) * gk_ref[...]
        if states:
            ks_ref, vs_ref = refs[pos], refs[pos + 1]
            ks_ref[0, rows] = k3.reshape(nk, tm).T if qknorm else y[:, nq:nq + nk]
            vs_ref[0, rows] = v
        if rope:
            tab = tab_ref[:, :, rows]
            q3 = _rope_t(q3, tab, HEAD_DIM // 4)
            k3 = _rope_t(k3, tab, HEAD_DIM // 4)
        qT_ref[0, :, rows] = (q3 * ATTN_Q_SCALE).reshape(nq, tm).astype(BF16)
        k_ref[0, rows] = k3.reshape(nk, tm).T.astype(BF16)
        vT_ref[0, sub_tile] = v.T.astype(BF16)


def _proj_gqa(x, mods, mod_row, g, w, tab, gq, gk, *, states):
    b, t, _ = x.shape
    tm = min(t, PROJ_STEP_ROWS)
    nq = N_HEADS * HEAD_DIM
    nk = KV_HEADS * HEAD_DIM
    rope = tab is not None
    qknorm = gq is not None
    in_specs = [pl.BlockSpec((1, tm, D_MODEL), lambda i, j: (i, j, 0)),
                pl.BlockSpec((1, 1, 6 * D_MODEL), lambda i, j: (mod_row(i), 0, 0)),
                pl.BlockSpec((1, D_MODEL), lambda i, j: (0, 0)),
                pl.BlockSpec((D_MODEL, QKV_DIM), lambda i, j: (0, 0))]
    args = [x, mods, g.reshape(1, D_MODEL), w.astype(BF16)]
    if rope:
        in_specs.append(pl.BlockSpec((4, HEAD_DIM // 4, tm), lambda i, j: (0, 0, j)))
        args.append(tab)
    if qknorm:
        in_specs += [pl.BlockSpec((HEAD_DIM, 1), lambda i, j: (0, 0))] * 2
        args += [gq.reshape(HEAD_DIM, 1), gk.reshape(HEAD_DIM, 1)]
    out_specs = [pl.BlockSpec((1, nq, tm), lambda i, j: (i, 0, j)),
                 pl.BlockSpec((1, tm, nk), lambda i, j: (i, j, 0)),
                 pl.BlockSpec((1, tm // ROW_TILE, nk, ROW_TILE), lambda i, j: (i, j, 0, 0))]
    out_shape = [jax.ShapeDtypeStruct((b, nq, t), BF16),
                 jax.ShapeDtypeStruct((b, t, nk), BF16),
                 jax.ShapeDtypeStruct((b, t // ROW_TILE, nk, ROW_TILE), BF16)]
    if states:
        out_specs += [pl.BlockSpec((1, tm, nk), lambda i, j: (i, j, 0))] * 2
        out_shape += [jax.ShapeDtypeStruct((b, t, nk), F32)] * 2
    return pl.pallas_call(
        functools.partial(_proj_gqa_kernel, rope=rope, qknorm=qknorm, states=states),
        grid=(b, t // tm), in_specs=in_specs, out_specs=out_specs, out_shape=out_shape,
        compiler_params=_params(2), name="proj_gqa",
    )(*args)


def _proj_mla_kernel(*refs, rope, states):
    x_ref, mod_ref, g_ref, wdq_ref, gq_ref, wuq_ref, wdkv_ref, gkv_ref = refs[:8]
    pos = 8
    if rope:
        tab_ref = refs[pos]
        pos += 1
    qT_ref, ck_ref = refs[pos], refs[pos + 1]
    pos += 2
    tm = x_ref.shape[1]
    quarter = ROPE_DIM // 4

    mod = mod_ref[0]
    h = _norm_mod(x_ref[0], g_ref[...], mod[:, 0:D_MODEL], mod[:, D_MODEL:2 * D_MODEL]).astype(BF16)
    qn = _rmsnorm(_dot(h, wdq_ref[...]), gq_ref[...])
    q3 = _dot(qn.astype(BF16), wuq_ref[...]).T.reshape(N_HEADS, MLA_HEAD_PAD, tm)
    ckv = _dot(h, wdkv_ref[...])
    cn = _rmsnorm(ckv[:, :KV_LORA], gkv_ref[...])
    kr = ckv[:, KV_LORA:]
    if states:
        cs_ref, krs_ref = refs[pos], refs[pos + 1]
        cs_ref[0] = cn
        krs_ref[0] = kr[:, :ROPE_DIM]
    if rope:
        tab = tab_ref[...]
        q_rot = _rope_t(q3[:, NOPE_DIM:NOPE_DIM + ROPE_DIM], tab, quarter)
        q3 = jnp.concatenate([q3[:, :NOPE_DIM], q_rot, q3[:, NOPE_DIM + ROPE_DIM:]], axis=1)
        krT = kr.T
        kr_rot = _rope_t(krT[:ROPE_DIM].reshape(1, ROPE_DIM, tm), tab, quarter).reshape(ROPE_DIM, tm)
        kr = jnp.concatenate([kr_rot, krT[ROPE_DIM:]], axis=0).T
    qT_ref[0] = (q3 * MLA_Q_SCALE).reshape(N_HEADS * MLA_HEAD_PAD, tm).astype(BF16)
    ck_ref[0] = jnp.concatenate([cn, kr], axis=1).astype(BF16)


def _proj_mla(x, mods, mod_row, g, wdq, gq, wuq_pad, wdkv_pad, gkv, tab, *, states):
    b, t, _ = x.shape
    tm = ROW_TILE
    rope = tab is not None
    nqp = N_HEADS * MLA_HEAD_PAD
    const = lambda i, j: (0, 0)
    in_specs = [pl.BlockSpec((1, tm, D_MODEL), lambda i, j: (i, j, 0)),
                pl.BlockSpec((1, 1, 6 * D_MODEL), lambda i, j: (mod_row(i), 0, 0)),
                pl.BlockSpec((1, D_MODEL), const),
                pl.BlockSpec((D_MODEL, Q_LORA), const),
                pl.BlockSpec((1, Q_LORA), const),
                pl.BlockSpec((Q_LORA, nqp), const),
                pl.BlockSpec((D_MODEL, CKR_PAD), const),
                pl.BlockSpec((1, KV_LORA), const)]
    args = [x, mods, g.reshape(1, D_MODEL), wdq.astype(BF16), gq.reshape(1, Q_LORA), wuq_pad,
            wdkv_pad, gkv.reshape(1, KV_LORA)]
    if rope:
        in_specs.append(pl.BlockSpec((4, ROPE_DIM // 4, tm), lambda i, j: (0, 0, j)))
        args.append(tab)
    out_specs = [pl.BlockSpec((1, nqp, tm), lambda i, j: (i, 0, j)),
                 pl.BlockSpec((1, tm, CKR_PAD), lambda i, j: (i, j, 0))]
    out_shape = [jax.ShapeDtypeStruct((b, nqp, t), BF16),
                 jax.ShapeDtypeStruct((b, t, CKR_PAD), BF16)]
    if states:
        out_specs += [pl.BlockSpec((1, tm, KV_LORA), lambda i, j: (i, j, 0)),
                      pl.BlockSpec((1, tm, ROPE_DIM), lambda i, j: (i, j, 0))]
        out_shape += [jax.ShapeDtypeStruct((b, t, KV_LORA), F32),
                      jax.ShapeDtypeStruct((b, t, ROPE_DIM), F32)]
    return pl.pallas_call(
        functools.partial(_proj_mla_kernel, rope=rope, states=states),
        grid=(b, t // tm), in_specs=in_specs, out_specs=out_specs, out_shape=out_shape,
        compiler_params=_params(2), name="proj_mla",
    )(*args)


def _expand_kernel(ck_ref, wk_ref, wv_ref, k_ref, vT_ref):
    ck = ck_ref[0]
    k_ref[0] = _dot(ck, wk_ref[...]).astype(BF16)
    vT_ref[0, 0] = _dot(ck[:, :KV_LORA], wv_ref[...]).T.astype(BF16)


def _expand_mla(ck, wk_exp, wv):
    b, t, _ = ck.shape
    tm = ROW_TILE
    nkp = N_HEADS * MLA_HEAD_PAD
    nv = N_HEADS * V_DIM
    return pl.pallas_call(
        _expand_kernel,
        grid=(b, t // tm),
        in_specs=[pl.BlockSpec((1, tm, CKR_PAD), lambda i, j: (i, j, 0)),
                  pl.BlockSpec((CKR_PAD, nkp), lambda i, j: (0, 0)),
                  pl.BlockSpec((KV_LORA, nv), lambda i, j: (0, 0))],
        out_specs=[pl.BlockSpec((1, tm, nkp), lambda i, j: (i, j, 0)),
                   pl.BlockSpec((1, 1, nv, tm), lambda i, j: (i, j, 0, 0))],
        out_shape=[jax.ShapeDtypeStruct((b, t, nkp), BF16),
                   jax.ShapeDtypeStruct((b, t // tm, nv, tm), BF16)],
        compiler_params=_params(2), name="expand_mla",
    )(ck, wk_exp, wv)


def _q_operand(q_ref, u, lo, width, group):
    qg = jnp.concatenate(
        [q_ref[0, (u * group + g) * HEAD_DIM:(u * group + g + 1) * HEAD_DIM, lo:lo + width] for g in range(group)],
        axis=1)
    z = jnp.zeros_like(qg)
    return jnp.concatenate([qg, z] if u % 2 == 0 else [z, qg], axis=0)


def _attn_dense_kernel(*refs, mla, units, tq, n_blocks, has_sink):
    q_ref, k_ref, v_ref = refs[:3]
    pos = 3
    sink_ref = None
    if has_sink:
        sink_ref = refs[pos]
        pos += 1
    o_ref = refs[pos]
    pos += 1
    tile_blocks = min(TILE_BLOCKS, n_blocks)
    nbuf = min(TILE_GROUP, pl.cdiv(n_blocks, tile_blocks))
    s_refs = [refs[pos + u * nbuf:pos + (u + 1) * nbuf] for u in range(units)]
    pos += units * nbuf
    p_refs = [refs[pos + u * nbuf:pos + (u + 1) * nbuf] for u in range(units)]
    pos += units * nbuf
    if mla:
        acc_ref, m_ref = refs[pos:]
        group = 1
    else:
        qp_ref, acc_ref, m_ref = refs[pos:]
        group = GROUP
        for u in range(units):
            qp_ref[u] = _q_operand(q_ref, u, 0, tq, group)
    ng = group * tq
    tk = KV_TILE
    cw = SOFTMAX_CHUNK
    sub = SUBLANES
    v_tiles = V_DIM // sub

    for u in range(units):
        m_ref[u] = jnp.broadcast_to(sink_ref[u], (sub, ng)) if has_sink else jnp.full((sub, ng), NEG_INF, F32)
        acc_ref[u, :v_tiles] = jnp.zeros((v_tiles, sub, ng), F32)
        acc_ref[u, v_tiles:] = jnp.full((SUM_ROWS // sub, sub, ng), 1.0 if has_sink else 0.0, F32)

    dz = jnp.minimum(pl.program_id(2), 0)

    def scores(b0, nb, par):
        rows = nb * tk
        off = b0 * tk if isinstance(b0, int) else pl.multiple_of(b0 * tk, tk)
        for u in range(units):
            if mla:
                kp = k_ref[0, pl.ds(off, rows), u * MLA_HEAD_PAD:(u + 1) * MLA_HEAD_PAD]
                qop = q_ref[0, u * MLA_HEAD_PAD:(u + 1) * MLA_HEAD_PAD, :]
            else:
                kp = k_ref[0, pl.ds(off, rows), (u // 2) * 2 * HEAD_DIM:(u // 2 + 1) * 2 * HEAD_DIM]
                qop = qp_ref[u]
            s_refs[u][par][dz, :rows // sub] = _dot(kp, qop).reshape(rows // sub, sub, ng)

    def accumulate(b0, nb, par):
        rows = nb * tk
        for u in range(units):
            s_ref, p_ref = s_refs[u][par], p_refs[u][par]
            m_blk = jnp.max(s_ref[dz, :rows // sub], axis=0)
            for shift in (4, 2, 1):
                m_blk = jnp.maximum(m_blk, pltpu.roll(m_blk, shift, 0))
            m_prev = m_ref[u]
            m_new = jnp.maximum(m_prev, m_blk)
            m_ref[u] = m_new
            alpha = jnp.exp2(m_prev - m_new)
            for c in range(ng // cw):
                cols = slice(c * cw, (c + 1) * cw)
                p = jnp.exp2(s_ref[dz, :rows // sub, :, cols] - m_ref[u, :, cols])
                p_ref[:rows, cols] = p.reshape(rows, cw).astype(BF16)
            v_rows = [v_ref[0, b0 + i, u * V_DIM:(u + 1) * V_DIM, :] for i in range(nb)]
            v_ones = jnp.concatenate([jnp.concatenate(v_rows, axis=1), jnp.ones((SUM_ROWS, rows), BF16)], axis=0)
            pv = _dot(v_ones, p_ref[:rows]).reshape(v_tiles + SUM_ROWS // sub, sub, ng)
            acc_ref[u] = alpha * acc_ref[u] + pv

    def tile_group(b0, sizes):
        starts = [b0 + sum(sizes[:r]) for r in range(len(sizes))]
        for r, nb in enumerate(sizes):
            scores(starts[r], nb, r)
        for r, nb in enumerate(sizes):
            accumulate(starts[r], nb, r)

    tiles = [n_blocks % tile_blocks] * bool(n_blocks % tile_blocks) + [tile_blocks] * (n_blocks // tile_blocks)
    head = len(tiles) % nbuf
    if head == 0 and tiles[0] != tile_blocks:
        head = nbuf
    if head:
        tile_group(0, tiles[:head])
    first = sum(tiles[:head])
    n_loop = (len(tiles) - head) // nbuf
    if n_loop == 1:
        tile_group(first, tiles[head:])
    elif n_loop > 1:
        def body(i, carry):
            tile_group(first + nbuf * tile_blocks * i, [tile_blocks] * nbuf)
            return carry
        lax.fori_loop(0, n_loop, body, 0)

    for u in range(units):
        o = (acc_ref[u, :v_tiles] / acc_ref[u, v_tiles]).reshape(V_DIM, ng)
        if mla:
            o_ref[0, u * V_DIM:(u + 1) * V_DIM, :] = o.astype(BF16)
        else:
            for g in range(group):
                r0 = (u * group + g) * HEAD_DIM
                o_ref[0, r0:r0 + HEAD_DIM, :] = o[:, g * tq:(g + 1) * tq].astype(BF16)


def _attn_dense(qT, k, vT, sink_cols, *, mla, tq, pairs=1):
    b, _, t = qT.shape
    tk = KV_TILE
    n = k.shape[1]
    n_pairs = N_HEADS // 2 if mla else KV_HEADS // 2
    units = 2 * pairs
    q_rows = pairs * (2 * MLA_HEAD_PAD if mla else 2 * GROUP * HEAD_DIM)
    o_rows = pairs * (2 * V_DIM if mla else 2 * GROUP * HEAD_DIM)
    k_cols = pairs * (2 * MLA_HEAD_PAD if mla else 2 * HEAD_DIM)
    ng = tq if mla else GROUP * tq
    tile_blocks = min(TILE_BLOCKS, n // tk)
    nbuf = min(TILE_GROUP, pl.cdiv(n // tk, tile_blocks))
    tile_rows = tile_blocks * tk
    in_specs = [pl.BlockSpec((1, q_rows, tq), lambda i, p, j: (i, p, j)),
                pl.BlockSpec((1, n, k_cols), lambda i, p, j: (i, 0, p)),
                pl.BlockSpec((1, n // tk, units * V_DIM, tk), lambda i, p, j: (i, 0, p, 0))]
    args = [qT, k, vT]
    if sink_cols is not None:
        in_specs.append(pl.BlockSpec((units, 1, ng), lambda i, p, j: (p, 0, 0)))
        args.append(sink_cols)
    sub = SUBLANES
    scratch = [pltpu.VMEM((1, tile_rows // sub, sub, ng), F32)] * (units * nbuf)
    scratch += [pltpu.VMEM((tile_rows, ng), BF16)] * (units * nbuf)
    if not mla:
        scratch.append(pltpu.VMEM((units, 2 * HEAD_DIM, ng), BF16))
    scratch += [pltpu.VMEM((units, (V_DIM + SUM_ROWS) // sub, sub, ng), F32),
                pltpu.VMEM((units, sub, ng), F32)]
    return pl.pallas_call(
        functools.partial(_attn_dense_kernel, mla=mla, units=units, tq=tq, n_blocks=n // tk,
                          has_sink=sink_cols is not None),
        grid=(b, n_pairs // pairs, t // tq), in_specs=in_specs,
        out_specs=pl.BlockSpec((1, o_rows, tq), lambda i, p, j: (i, p, j)),
        out_shape=jax.ShapeDtypeStruct((b, N_HEADS * V_DIM, t), BF16),
        scratch_shapes=scratch, compiler_params=_params(3),
        name="attn_mla" if mla else "attn_gqa",
    )(*args)


def _attn_window_kernel(q_ref, kc_ref, vc_ref, kp_ref, vp_ref, kcur_ref, vcur_ref, kn_ref, vn_ref,
                        sink_ref, o_ref, *scratch, tq, n_steps):
    wb = WINDOW_BLOCK
    step = pl.program_id(2)
    n_sub = tq // wb
    s_refs, p_refs = scratch[:2 * n_sub], scratch[2 * n_sub:]
    ng = GROUP * wb
    k_off = lax.broadcasted_iota(jnp.int32, (wb, ng), 0)
    q_off = lax.broadcasted_iota(jnp.int32, (wb, ng), 1) % wb
    keep_prev = k_off >= q_off
    keep_next = k_off <= q_off
    first = step == 0
    last = step == n_steps - 1
    n_ctx = kc_ref.shape[1]
    ones_rows = jnp.ones((SUM_ROWS, n_ctx + 3 * wb), BF16)

    def kv_block(i):
        if i < 0:
            return kp_ref[0], vp_ref[0, 0]
        if i >= n_sub:
            return kn_ref[0], vn_ref[0, 0]
        c, r = divmod(i * wb, KV_TILE)
        return kcur_ref[0, i * wb:(i + 1) * wb, :], vcur_ref[0, c, :, r:r + wb]

    sub = SUBLANES
    n_keys = n_ctx + 3 * wb
    v_tiles = V_DIM // sub

    def tiles(x):
        return x.reshape(x.shape[0] // sub, sub, x.shape[1])

    dz = jnp.minimum(step, 0)

    for i in range(n_sub):
        kcat = jnp.concatenate([kc_ref[0], kv_block(i - 1)[0], kv_block(i)[0], kv_block(i + 1)[0]], axis=0)
        keep0 = jnp.logical_and(keep_prev, jnp.logical_not(first)) if i == 0 else keep_prev
        keep2 = jnp.logical_and(keep_next, jnp.logical_not(last)) if i == n_sub - 1 else keep_next
        for u in range(2):
            s = _dot(kcat, _q_operand(q_ref, u, i * wb, wb, GROUP))
            s_ref = s_refs[2 * i + u]
            t0, t1, t2 = n_ctx // sub, (n_ctx + wb) // sub, (n_ctx + 2 * wb) // sub
            s_ref[dz, :t0] = tiles(s[:n_ctx])
            s_ref[dz, t0:t1] = tiles(jnp.where(keep0, s[n_ctx:n_ctx + wb], NEG_INF))
            s_ref[dz, t1:t2] = tiles(s[n_ctx + wb:n_ctx + 2 * wb])
            s_ref[dz, t2:] = tiles(jnp.where(keep2, s[n_ctx + 2 * wb:], NEG_INF))

    for i in range(n_sub):
        vcat = jnp.concatenate([vc_ref[0, 0], kv_block(i - 1)[1], kv_block(i)[1], kv_block(i + 1)[1]], axis=1)
        for u in range(2):
            s_ref, p_ref = s_refs[2 * i + u], p_refs[2 * i + u]
            sink = jnp.broadcast_to(sink_ref[u], (sub, ng))
            m = jnp.max(s_ref[dz], axis=0)
            for shift in (4, 2, 1):
                m = jnp.maximum(m, pltpu.roll(m, shift, 0))
            m = jnp.maximum(m, sink)
            p_ref[...] = jnp.exp2(s_ref[dz] - m).reshape(n_keys, ng).astype(BF16)
            v_ones = jnp.concatenate([vcat[u * V_DIM:(u + 1) * V_DIM, :], ones_rows], axis=0)
            r = tiles(_dot(v_ones, p_ref[...]))
            o = (r[:v_tiles] / (r[v_tiles] + jnp.exp2(sink - m))).reshape(V_DIM, ng)
            for g in range(GROUP):
                r0 = (u * GROUP + g) * HEAD_DIM
                o_ref[0, r0:r0 + HEAD_DIM, i * wb:(i + 1) * wb] = o[:, g * wb:(g + 1) * wb].astype(BF16)


def _attn_window(qT, k_ctx, vT_ctx, k, vT, sink_cols, *, tq):
    b, _, t = qT.shape
    wb = WINDOW_BLOCK
    n_ctx = k_ctx.shape[1]
    n_steps = t // tq
    n_sub = tq // wb
    n_wb = t // wb
    per_tile = KV_TILE // wb
    pair_rows = 2 * GROUP * HEAD_DIM
    ng = GROUP * wb

    def prev_blk(j):
        return jnp.maximum(j * n_sub - 1, 0)

    def next_blk(j):
        return jnp.minimum((j + 1) * n_sub, n_wb - 1)

    in_specs = [
        pl.BlockSpec((1, pair_rows, tq), lambda i, p, j: (i, p, j)),
        pl.BlockSpec((1, n_ctx, 2 * HEAD_DIM), lambda i, p, j: (i, 0, p)),
        pl.BlockSpec((1, n_ctx // KV_TILE, 2 * V_DIM, KV_TILE), lambda i, p, j: (i, 0, p, 0)),
        pl.BlockSpec((1, wb, 2 * HEAD_DIM), lambda i, p, j: (i, prev_blk(j), p)),
        pl.BlockSpec((1, 1, 2 * V_DIM, wb), lambda i, p, j: (i, prev_blk(j) // per_tile, p, prev_blk(j) % per_tile)),
        pl.BlockSpec((1, tq, 2 * HEAD_DIM), lambda i, p, j: (i, j, p)),
        pl.BlockSpec((1, tq // KV_TILE, 2 * V_DIM, KV_TILE), lambda i, p, j: (i, j, p, 0)),
        pl.BlockSpec((1, wb, 2 * HEAD_DIM), lambda i, p, j: (i, next_blk(j), p)),
        pl.BlockSpec((1, 1, 2 * V_DIM, wb), lambda i, p, j: (i, next_blk(j) // per_tile, p, next_blk(j) % per_tile)),
        pl.BlockSpec((2, 1, ng), lambda i, p, j: (p, 0, 0)),
    ]
    return pl.pallas_call(
        functools.partial(_attn_window_kernel, tq=tq, n_steps=n_steps),
        grid=(b, KV_HEADS // 2, n_steps), in_specs=in_specs,
        out_specs=pl.BlockSpec((1, pair_rows, tq), lambda i, p, j: (i, p, j)),
        out_shape=jax.ShapeDtypeStruct((b, N_HEADS * HEAD_DIM, t), BF16),
        scratch_shapes=([pltpu.VMEM((1, (n_ctx + 3 * wb) // SUBLANES, SUBLANES, ng), F32)] * (2 * n_sub)
                        + [pltpu.VMEM((n_ctx + 3 * wb, ng), BF16)] * (2 * n_sub)),
        compiler_params=_params(3), name="attn_window",
    )(qT, k_ctx, vT_ctx, k, vT, k, vT, k, vT, sink_cols)


def _post_kernel(*refs, final):
    x_ref, oT_ref, mod_ref, g2_ref, wo_ref, win_ref, wout_ref = refs[:7]
    pos = 7
    if final:
        gf_ref = refs[pos]
        pos += 1
    out_ref = refs[pos]
    mod = mod_ref[0]
    gate, shift2, scale2, gate2 = (mod[:, c * D_MODEL:(c + 1) * D_MODEL] for c in range(2, 6))
    attn = lax.dot_general(oT_ref[0], wo_ref[...], (((0,), (0,)), ((), ())), preferred_element_type=F32)
    x1 = x_ref[0] + gate * attn
    h2 = _norm_mod(x1, g2_ref[...], shift2, scale2).astype(BF16)
    acc = jnp.zeros_like(x1)
    for c in range(D_FF // FF_CHUNK):
        u = jnp.maximum(_dot(h2, win_ref[:, c * FF_CHUNK:(c + 1) * FF_CHUNK]), 0.0)
        acc = acc + _dot((u * u).astype(BF16), wout_ref[c * FF_CHUNK:(c + 1) * FF_CHUNK, :])
    x2 = x1 + gate2 * acc
    out_ref[0] = _rmsnorm(x2, gf_ref[...]) if final else x2


def _post(x, oT, mods, mod_row, g2, wo, win, wout, g_final):
    b, t, _ = x.shape
    tm = min(t, MLP_ROW_TILE)
    final = g_final is not None
    const = lambda i, j: (0, 0)
    in_specs = [pl.BlockSpec((1, tm, D_MODEL), lambda i, j: (i, j, 0)),
                pl.BlockSpec((1, D_MODEL, tm), lambda i, j: (i, 0, j)),
                pl.BlockSpec((1, 1, 6 * D_MODEL), lambda i, j: (mod_row(i), 0, 0)),
                pl.BlockSpec((1, D_MODEL), const),
                pl.BlockSpec((D_MODEL, D_MODEL), const, pipeline_mode=pl.Buffered(1)),
                pl.BlockSpec((D_MODEL, D_FF), const, pipeline_mode=pl.Buffered(1)),
                pl.BlockSpec((D_FF, D_MODEL), const, pipeline_mode=pl.Buffered(1))]
    args = [x, oT, mods, g2.reshape(1, D_MODEL), wo, win, wout]
    if final:
        in_specs.append(pl.BlockSpec((1, D_MODEL), const))
        args.append(g_final.reshape(1, D_MODEL))
    return pl.pallas_call(
        functools.partial(_post_kernel, final=final),
        grid=(b, t // tm), in_specs=in_specs,
        out_specs=pl.BlockSpec((1, tm, D_MODEL), lambda i, j: (i, j, 0)),
        out_shape=jax.ShapeDtypeStruct((b, t, D_MODEL), F32),
        compiler_params=_params(2), name="post_mlp",
    )(*args)


def _sink_cols(sink, width):
    return jnp.repeat((sink.astype(F32) * LOG2_E).reshape(KV_HEADS, 1, GROUP), width, axis=2)


def _mla_weights(w_uq, w_dkv, w_ukv):
    wuq_pad = jnp.pad(w_uq.reshape(Q_LORA, N_HEADS, NOPE_DIM + ROPE_DIM),
                      ((0, 0), (0, 0), (0, MLA_HEAD_PAD - NOPE_DIM - ROPE_DIM)))
    wuq_pad = wuq_pad.reshape(Q_LORA, N_HEADS * MLA_HEAD_PAD).astype(BF16)
    wdkv_pad = jnp.pad(w_dkv, ((0, 0), (0, CKR_PAD - KV_LORA - ROPE_DIM))).astype(BF16)
    w3 = w_ukv.reshape(KV_LORA, N_HEADS, NOPE_DIM + V_DIM)
    wk_nope = jnp.pad(w3[:, :, :NOPE_DIM], ((0, 0), (0, 0), (0, MLA_HEAD_PAD - NOPE_DIM)))
    place = jnp.pad(jnp.eye(ROPE_DIM, dtype=F32), ((0, 0), (NOPE_DIM, MLA_HEAD_PAD - NOPE_DIM - ROPE_DIM)))
    place = jnp.broadcast_to(place[:, None, :], (ROPE_DIM, N_HEADS, MLA_HEAD_PAD))
    wk_exp = jnp.concatenate([wk_nope.reshape(KV_LORA, -1), place.reshape(ROPE_DIM, -1),
                              jnp.zeros((CKR_PAD - KV_LORA - ROPE_DIM, N_HEADS * MLA_HEAD_PAD), F32)], axis=0)
    wv = w3[:, :, NOPE_DIM:].reshape(KV_LORA, N_HEADS * V_DIM)
    return wuq_pad, wdkv_pad, wk_exp.astype(BF16), wv.astype(BF16)


def _ctx_gqa(cache_k, cache_v):
    b, p = cache_k.shape[:2]
    k = cache_k.reshape(b, p, KV_HEADS * HEAD_DIM).astype(BF16)
    v = cache_v.reshape(b, p // KV_TILE, KV_TILE, KV_HEADS * HEAD_DIM)
    return k, jnp.swapaxes(v, 2, 3).astype(BF16)


def kernel(x_prompt, x_sample, c, cache_a_k, cache_a_v, cache_b_ckv, cache_b_krope, cache_c_k, cache_c_v, c_ctx, w_ada, b_ada, norm_g, w_mlp_in, w_mlp_out, a_w_qkv, a_sink, a_w_o, b_w_dq, b_g_q, b_w_uq, b_w_dkv, b_g_kv, b_w_ukv, b_w_o, c_w_qkv, c_g_q, c_g_k, c_w_o, g_final):
    n_batch, seq = x_prompt.shape[:2]
    n_dec, dec_seq = x_sample.shape[:2]
    cond = jnp.concatenate([c_ctx[None, :], c, jnp.zeros((8 - 1 - n_dec, D_MODEL), F32)], axis=0)
    mods_all = _ada_all(cond, w_ada, b_ada)
    tab64 = _rope_tables(dec_seq, HEAD_DIM)
    tab32 = _rope_tables(dec_seq, ROPE_DIM)
    prompt_row = lambda i: 0
    sample_row = lambda i: i + 1

    xp, xs = x_prompt, x_sample
    st = {"a_k": [], "a_v": [], "b_ckv": [], "b_kr": [], "c_k": [], "c_v": []}
    for layer in range(DEPTH):
        kind, j = layer % 3, layer // 3
        mods = mods_all[layer].reshape(8, 1, 6 * D_MODEL)
        g1, g2 = norm_g[layer, 0], norm_g[layer, 1]
        if kind == 0:
            sink = a_sink[j]
            qT, k, vT, ks, vs = _proj_gqa(xp, mods, prompt_row, g1, a_w_qkv[j], None, None, None, states=True)
            st["a_k"].append(ks)
            st["a_v"].append(vs)
            oT_p = _attn_dense(qT, k, vT, _sink_cols(sink, seq), mla=False, tq=seq, pairs=KV_HEADS // 2)
            qT, k, vT = _proj_gqa(xs, mods, sample_row, g1, a_w_qkv[j], tab64, None, None, states=False)
            k_ctx, vT_ctx = _ctx_gqa(cache_a_k[:, j], cache_a_v[:, j])
            oT_s = _attn_window(qT, k_ctx, vT_ctx, k, vT, _sink_cols(sink, WINDOW_BLOCK), tq=512)
            wo = a_w_o[j]
        elif kind == 1:
            wuq_pad, wdkv_pad, wk_exp, wv = _mla_weights(b_w_uq[j], b_w_dkv[j], b_w_ukv[j])
            qT, ck, cs, krs = _proj_mla(xp, mods, prompt_row, g1, b_w_dq[j], b_g_q[j], wuq_pad, wdkv_pad,
                                        b_g_kv[j], None, states=True)
            st["b_ckv"].append(cs)
            st["b_kr"].append(krs)
            oT_p = _attn_dense(qT, *_expand_mla(ck, wk_exp, wv), None, mla=True, tq=seq, pairs=N_HEADS // 4)
            qT, ck = _proj_mla(xs, mods, sample_row, g1, b_w_dq[j], b_g_q[j], wuq_pad, wdkv_pad,
                               b_g_kv[j], tab32, states=False)
            ck_ctx = jnp.concatenate(
                [cache_b_ckv[:, j], cache_b_krope[:, j],
                 jnp.zeros(cache_b_krope.shape[:1] + cache_b_krope.shape[2:3] + (CKR_PAD - KV_LORA - ROPE_DIM,), F32)],
                axis=-1).astype(BF16)
            ck_all = jnp.concatenate([ck_ctx, ck], axis=1)
            oT_s = _attn_dense(qT, *_expand_mla(ck_all, wk_exp, wv), None, mla=True, tq=1024)
            wo = b_w_o[j]
        else:
            qT, k, vT, ks, vs = _proj_gqa(xp, mods, prompt_row, g1, c_w_qkv[j], None, c_g_q[j], c_g_k[j],
                                          states=True)
            st["c_k"].append(ks)
            st["c_v"].append(vs)
            oT_p = _attn_dense(qT, k, vT, None, mla=False, tq=seq, pairs=KV_HEADS // 2)
            qT, k, vT = _proj_gqa(xs, mods, sample_row, g1, c_w_qkv[j], tab64, c_g_q[j], c_g_k[j], states=False)
            k_ctx, vT_ctx = _ctx_gqa(cache_c_k[:, j], cache_c_v[:, j])
            oT_s = _attn_dense(qT, jnp.concatenate([k_ctx, k], axis=1), jnp.concatenate([vT_ctx, vT], axis=1),
                               None, mla=False, tq=512)
            wo = c_w_o[j]
        gf = g_final if layer == DEPTH - 1 else None
        wo, win, wout = wo.astype(BF16), w_mlp_in[layer].astype(BF16), w_mlp_out[layer].astype(BF16)
        xp = _post(xp, oT_p, mods, prompt_row, g2, wo, win, wout, gf)
        xs = _post(xs, oT_s, mods, sample_row, g2, wo, win, wout, gf)

    def stack_heads(parts):
        return jnp.stack([p.reshape(n_batch, seq, KV_HEADS, HEAD_DIM) for p in parts], axis=1)

    return (xp, xs, stack_heads(st["a_k"]), stack_heads(st["a_v"]),
            jnp.stack(st["b_ckv"], axis=1), jnp.stack(st["b_kr"], axis=1),
            stack_heads(st["c_k"]), stack_heads(st["c_v"]))
```

```python
import functools

import jax
import jax.numpy as jnp
from jax import lax
from jax.experimental import pallas as pl
from jax.experimental.pallas import tpu as pltpu

D_MODEL = 1024
DEPTH = 4
N_HEADS = 16
KV_HEADS = 4
GROUP = N_HEADS // KV_HEADS
HEAD_DIM = 64
QKV_DIM = (N_HEADS + 2 * KV_HEADS) * HEAD_DIM
Q_LORA = 384
KV_LORA = 256
NOPE_DIM = 64
ROPE_DIM = 32
V_DIM = 64
D_FF = 4 * D_MODEL
GRID_W = 64
WINDOW_BLOCK = 128
ROPE_THETA = 10000.0
EPS = 1e-6
NEG_INF = -1e30
LOG2_E = 1.4426950408889634
ATTN_Q_SCALE = HEAD_DIM ** -0.5 * LOG2_E
MLA_Q_SCALE = (NOPE_DIM + ROPE_DIM) ** -0.5 * LOG2_E

MLA_HEAD_PAD = 128
CKR_PAD = KV_LORA + 128
ROW_TILE = 256
PROJ_STEP_ROWS = 512
MLP_ROW_TILE = 512
KV_TILE = 256
SOFTMAX_CHUNK = 256
SUBLANES = 8
TILE_BLOCKS = 1
TILE_GROUP = 4
SUM_ROWS = 16
FF_CHUNK = 1024
V7X_VMEM_LIMIT_BYTES = 48 * 1024 * 1024

F32 = jnp.float32
BF16 = jnp.bfloat16


def _params(n_axes):
    return pltpu.CompilerParams(dimension_semantics=("arbitrary",) * n_axes,
                                vmem_limit_bytes=V7X_VMEM_LIMIT_BYTES)


def _dot(a, b):
    return jnp.dot(a, b, preferred_element_type=F32)


def _rmsnorm(x, g):
    ms = jnp.mean(x * x, axis=-1, keepdims=True)
    return x * lax.rsqrt(ms + EPS) * g


def _norm_mod(x, g, shift, scale):
    return _rmsnorm(x, g) * (1.0 + scale) + shift


def _ada_kernel(cond_ref, w_ref, b_ref, o_ref):
    cnd = cond_ref[...]
    act = cnd / (1.0 + jnp.exp(-cnd))
    o_ref[0] = _dot(act.astype(BF16), w_ref[0].astype(BF16)) + b_ref[0]


def _ada_all(cond, w_ada, b_ada):
    tn = 1536
    n = 6 * D_MODEL
    return pl.pallas_call(
        _ada_kernel,
        grid=(DEPTH, n // tn),
        in_specs=[pl.BlockSpec((8, D_MODEL), lambda l, j: (0, 0)),
                  pl.BlockSpec((1, D_MODEL, tn), lambda l, j: (l, 0, j)),
                  pl.BlockSpec((1, 1, tn), lambda l, j: (l, 0, j))],
        out_specs=pl.BlockSpec((1, 8, tn), lambda l, j: (l, 0, j)),
        out_shape=jax.ShapeDtypeStruct((DEPTH, 8, n), F32),
        compiler_params=_params(2),
        name="ada_mod",
    )(cond, w_ada, b_ada.reshape(DEPTH, 1, n))


def _rope_t(x3, tab, half):
    a, b = x3[:, 0:half], x3[:, half:2 * half]
    c, d = x3[:, 2 * half:3 * half], x3[:, 3 * half:4 * half]
    cr, sr, cc, sc = tab[0], tab[1], tab[2], tab[3]
    return jnp.concatenate([a * cr - b * sr, a * sr + b * cr, c * cc - d * sc, c * sc + d * cc], axis=1)


def _rope_tables(t_len, dim):
    half = dim // 2
    pos = jnp.arange(t_len)
    rows = (pos // GRID_W).astype(F32)
    cols = (pos % GRID_W).astype(F32)
    freqs = ROPE_THETA ** (-jnp.arange(0, half, 2, dtype=F32) / half)
    ang_r = freqs[:, None] * rows[None, :]
    ang_c = freqs[:, None] * cols[None, :]
    return jnp.stack([jnp.cos(ang_r), jnp.sin(ang_r), jnp.cos(ang_c), jnp.sin(ang_c)])


def _proj_gqa_kernel(*refs, rope, qknorm, states):
    x_ref, mod_ref, g_ref, w_ref = refs[:4]
    pos = 4
    if rope:
        tab_ref = refs[pos]
        pos += 1
    if qknorm:
        gq_ref, gk_ref = refs[pos], refs[pos + 1]
        pos += 2
    qT_ref, k_ref, vT_ref = refs[pos:pos + 3]
    pos += 3
    tm = ROW_TILE
    nq = N_HEADS * HEAD_DIM
    nk = KV_HEADS * HEAD_DIM

    mod = mod_ref[0]
    for sub_tile in range(x_ref.shape[1] // tm):
        rows = slice(sub_tile * tm, (sub_tile + 1) * tm)
        h = _norm_mod(x_ref[0, rows], g_ref[...], mod[:, 0:D_MODEL], mod[:, D_MODEL:2 * D_MODEL])
        y = _dot(h.astype(BF16), w_ref[...])
        q3 = y[:, :nq].T.reshape(N_HEADS, HEAD_DIM, tm)
        k3 = y[:, nq:nq + nk].T.reshape(KV_HEADS, HEAD_DIM, tm)
        v = y[:, nq + nk:]
        if qknorm:
            q3 = q3 * lax.rsqrt(jnp.mean(q3 * q3, axis=1, keepdims=True) + EPS) * gq_ref[...]
            k3 = k3 * lax.rsqrt(jnp.mean(k3 * k3, axis=1, keepdims=True) + EPS) * gk_ref[...]
        if states:
            ks_ref, vs_ref = refs[pos], refs[pos + 1]
            ks_ref[0, rows] = k3.reshape(nk, tm).T if qknorm else y[:, nq:nq + nk]
            vs_ref[0, rows] = v
        if rope:
            tab = tab_ref[:, :, rows]
            q3 = _rope_t(q3, tab, HEAD_DIM // 4)
            k3 = _rope_t(k3, tab, HEAD_DIM // 4)
        qT_ref[0, :, rows] = (q3 * ATTN_Q_SCALE).reshape(nq, tm).astype(BF16)
        k_ref[0, rows] = k3.reshape(nk, tm).T.astype(BF16)
        vT_ref[0, sub_tile] = v.T.astype(BF16)


def _proj_gqa(x, mods, mod_row, g, w, tab, gq, gk, *, states):
    b, t, _ = x.shape
    tm = min(t, PROJ_STEP_ROWS)
    nq = N_HEADS * HEAD_DIM
    nk = KV_HEADS * HEAD_DIM
    rope = tab is not None
    qknorm = gq is not None
    in_specs = [pl.BlockSpec((1, tm, D_MODEL), lambda i, j: (i, j, 0)),
                pl.BlockSpec((1, 1, 6 * D_MODEL), lambda i, j: (mod_row(i), 0, 0)),
                pl.BlockSpec((1, D_MODEL), lambda i, j: (0, 0)),
                pl.BlockSpec((D_MODEL, QKV_DIM), lambda i, j: (0, 0))]
    args = [x, mods, g.reshape(1, D_MODEL), w.astype(BF16)]
    if rope:
        in_specs.append(pl.BlockSpec((4, HEAD_DIM // 4, tm), lambda i, j: (0, 0, j)))
        args.append(tab)
    if qknorm:
        in_specs += [pl.BlockSpec((HEAD_DIM, 1), lambda i, j: (0, 0))] * 2
        args += [gq.reshape(HEAD_DIM, 1), gk.reshape(HEAD_DIM, 1)]
    out_specs = [pl.BlockSpec((1, nq, tm), lambda i, j: (i, 0, j)),
                 pl.BlockSpec((1, tm, nk), lambda i, j: (i, j, 0)),
                 pl.BlockSpec((1, tm // ROW_TILE, nk, ROW_TILE), lambda i, j: (i, j, 0, 0))]
    out_shape = [jax.ShapeDtypeStruct((b, nq, t), BF16),
                 jax.ShapeDtypeStruct((b, t, nk), BF16),
                 jax.ShapeDtypeStruct((b, t // ROW_TILE, nk, ROW_TILE), BF16)]
    if states:
        out_specs += [pl.BlockSpec((1, tm, nk), lambda i, j: (i, j, 0))] * 2
        out_shape += [jax.ShapeDtypeStruct((b, t, nk), F32)] * 2
    return pl.pallas_call(
        functools.partial(_proj_gqa_kernel, rope=rope, qknorm=qknorm, states=states),
        grid=(b, t // tm), in_specs=in_specs, out_specs=out_specs, out_shape=out_shape,
        compiler_params=_params(2), name="proj_gqa",
    )(*args)


def _proj_mla_kernel(*refs, rope, states):
    x_ref, mod_ref, g_ref, wdq_ref, gq_ref, wuq_ref, wdkv_ref, gkv_ref = refs[:8]
    pos = 8
    if rope:
        tab_ref = refs[pos]
        pos += 1
    qT_ref, ck_ref = refs[pos], refs[pos + 1]
    pos += 2
    tm = ROW_TILE
    quarter = ROPE_DIM // 4

    mod = mod_ref[0]
    for sub_tile in range(x_ref.shape[1] // tm):
        rows = slice(sub_tile * tm, (sub_tile + 1) * tm)
        h = _norm_mod(x_ref[0, rows], g_ref[...], mod[:, 0:D_MODEL], mod[:, D_MODEL:2 * D_MODEL]).astype(BF16)
        qn = _rmsnorm(_dot(h, wdq_ref[...]), gq_ref[...])
        q3 = _dot(qn.astype(BF16), wuq_ref[...]).T.reshape(N_HEADS, MLA_HEAD_PAD, tm)
        ckv = _dot(h, wdkv_ref[...])
        cn = _rmsnorm(ckv[:, :KV_LORA], gkv_ref[...])
        kr = ckv[:, KV_LORA:]
        if states:
            cs_ref, krs_ref = refs[pos], refs[pos + 1]
            cs_ref[0, rows] = cn
            krs_ref[0, rows] = kr[:, :ROPE_DIM]
        if rope:
            tab = tab_ref[:, :, rows]
            q_rot = _rope_t(q3[:, NOPE_DIM:NOPE_DIM + ROPE_DIM], tab, quarter)
            q3 = jnp.concatenate([q3[:, :NOPE_DIM], q_rot, q3[:, NOPE_DIM + ROPE_DIM:]], axis=1)
            krT = kr.T
            kr_rot = _rope_t(krT[:ROPE_DIM].reshape(1, ROPE_DIM, tm), tab, quarter).reshape(ROPE_DIM, tm)
            kr = jnp.concatenate([kr_rot, krT[ROPE_DIM:]], axis=0).T
        qT_ref[0, :, rows] = (q3 * MLA_Q_SCALE).reshape(N_HEADS * MLA_HEAD_PAD, tm).astype(BF16)
        ck_ref[0, rows] = jnp.concatenate([cn, kr], axis=1).astype(BF16)


def _proj_mla(x, mods, mod_row, g, wdq, gq, wuq_pad, wdkv_pad, gkv, tab, *, states):
    b, t, _ = x.shape
    tm = min(t, PROJ_STEP_ROWS)
    rope = tab is not None
    nqp = N_HEADS * MLA_HEAD_PAD
    const = lambda i, j: (0, 0)
    in_specs = [pl.BlockSpec((1, tm, D_MODEL), lambda i, j: (i, j, 0)),
                pl.BlockSpec((1, 1, 6 * D_MODEL), lambda i, j: (mod_row(i), 0, 0)),
                pl.BlockSpec((1, D_MODEL), const),
                pl.BlockSpec((D_MODEL, Q_LORA), const),
                pl.BlockSpec((1, Q_LORA), const),
                pl.BlockSpec((Q_LORA, nqp), const),
                pl.BlockSpec((D_MODEL, CKR_PAD), const),
                pl.BlockSpec((1, KV_LORA), const)]
    args = [x, mods, g.reshape(1, D_MODEL), wdq.astype(BF16), gq.reshape(1, Q_LORA), wuq_pad,
            wdkv_pad, gkv.reshape(1, KV_LORA)]
    if rope:
        in_specs.append(pl.BlockSpec((4, ROPE_DIM // 4, tm), lambda i, j: (0, 0, j)))
        args.append(tab)
    out_specs = [pl.BlockSpec((1, nqp, tm), lambda i, j: (i, 0, j)),
                 pl.BlockSpec((1, tm, CKR_PAD), lambda i, j: (i, j, 0))]
    out_shape = [jax.ShapeDtypeStruct((b, nqp, t), BF16),
                 jax.ShapeDtypeStruct((b, t, CKR_PAD), BF16)]
    if states:
        out_specs += [pl.BlockSpec((1, tm, KV_LORA), lambda i, j: (i, j, 0)),
                      pl.BlockSpec((1, tm, ROPE_DIM), lambda i, j: (i, j, 0))]
        out_shape += [jax.ShapeDtypeStruct((b, t, KV_LORA), F32),
                      jax.ShapeDtypeStruct((b, t, ROPE_DIM), F32)]
    return pl.pallas_call(
        functools.partial(_proj_mla_kernel, rope=rope, states=states),
        grid=(b, t // tm), in_specs=in_specs, out_specs=out_specs, out_shape=out_shape,
        compiler_params=_params(2), name="proj_mla",
    )(*args)


def _expand_kernel(ck_ref, wk_ref, wv_ref, k_ref, vT_ref):
    ck = ck_ref[0]
    k_ref[0] = _dot(ck, wk_ref[...]).astype(BF16)
    vT_ref[0, 0] = _dot(ck[:, :KV_LORA], wv_ref[...]).T.astype(BF16)


def _expand_mla(ck, wk_exp, wv):
    b, t, _ = ck.shape
    tm = ROW_TILE
    nkp = N_HEADS * MLA_HEAD_PAD
    nv = N_HEADS * V_DIM
    return pl.pallas_call(
        _expand_kernel,
        grid=(b, t // tm),
        in_specs=[pl.BlockSpec((1, tm, CKR_PAD), lambda i, j: (i, j, 0)),
                  pl.BlockSpec((CKR_PAD, nkp), lambda i, j: (0, 0)),
                  pl.BlockSpec((KV_LORA, nv), lambda i, j: (0, 0))],
        out_specs=[pl.BlockSpec((1, tm, nkp), lambda i, j: (i, j, 0)),
                   pl.BlockSpec((1, 1, nv, tm), lambda i, j: (i, j, 0, 0))],
        out_shape=[jax.ShapeDtypeStruct((b, t, nkp), BF16),
                   jax.ShapeDtypeStruct((b, t // tm, nv, tm), BF16)],
        compiler_params=_params(2), name="expand_mla",
    )(ck, wk_exp, wv)


def _q_operand(q_ref, u, lo, width, group):
    qg = jnp.concatenate(
        [q_ref[0, (u * group + g) * HEAD_DIM:(u * group + g + 1) * HEAD_DIM, lo:lo + width] for g in range(group)],
        axis=1)
    z = jnp.zeros_like(qg)
    return jnp.concatenate([qg, z] if u % 2 == 0 else [z, qg], axis=0)


def _attn_dense_kernel(*refs, mla, units, tq, n_blocks, has_sink):
    q_ref, k_ref, v_ref = refs[:3]
    pos = 3
    sink_ref = None
    if has_sink:
        sink_ref = refs[pos]
        pos += 1
    o_ref = refs[pos]
    pos += 1
    tile_blocks = min(TILE_BLOCKS, n_blocks)
    nbuf = min(TILE_GROUP, pl.cdiv(n_blocks, tile_blocks))
    s_refs = [refs[pos + u * nbuf:pos + (u + 1) * nbuf] for u in range(units)]
    pos += units * nbuf
    p_refs = [refs[pos + u * nbuf:pos + (u + 1) * nbuf] for u in range(units)]
    pos += units * nbuf
    if mla:
        acc_ref, m_ref = refs[pos:]
        group = 1
    else:
        qp_ref, acc_ref, m_ref = refs[pos:]
        group = GROUP
        for u in range(units):
            qp_ref[u] = _q_operand(q_ref, u, 0, tq, group)
    ng = group * tq
    tk = KV_TILE
    cw = SOFTMAX_CHUNK
    sub = SUBLANES
    v_tiles = V_DIM // sub

    for u in range(units):
        m_ref[u] = jnp.broadcast_to(sink_ref[u], (sub, ng)) if has_sink else jnp.full((sub, ng), NEG_INF, F32)
        acc_ref[u, :v_tiles] = jnp.zeros((v_tiles, sub, ng), F32)
        acc_ref[u, v_tiles:] = jnp.full((SUM_ROWS // sub, sub, ng), 1.0 if has_sink else 0.0, F32)

    def scores(b0, nb, par):
        rows = nb * tk
        off = b0 * tk if isinstance(b0, int) else pl.multiple_of(b0 * tk, tk)
        for u in range(units):
            if mla:
                kp = k_ref[0, u // 2, pl.ds(off, rows), (u % 2) * MLA_HEAD_PAD:(u % 2 + 1) * MLA_HEAD_PAD]
                qop = q_ref[0, u * MLA_HEAD_PAD:(u + 1) * MLA_HEAD_PAD, :]
            else:
                kp = k_ref[0, u // 2, pl.ds(off, rows), :]
                qop = qp_ref[u]
            s_refs[u][par][:rows // sub] = _dot(kp, qop).reshape(rows // sub, sub, ng)

    def accumulate(b0, nb, par):
        rows = nb * tk
        for u in range(units):
            s_ref, p_ref = s_refs[u][par], p_refs[u][par]
            m_blk = jnp.max(s_ref[:rows // sub], axis=0)
            for shift in (4, 2, 1):
                m_blk = jnp.maximum(m_blk, pltpu.roll(m_blk, shift, 0))
            m_prev = m_ref[u]
            m_new = jnp.maximum(m_prev, m_blk)
            m_ref[u] = m_new
            alpha = jnp.exp2(m_prev - m_new)
            for c in range(ng // cw):
                cols = slice(c * cw, (c + 1) * cw)
                p = jnp.exp2(s_ref[:rows // sub, :, cols] - m_ref[u, :, cols])
                p_ref[:rows, cols] = p.reshape(rows, cw).astype(BF16)
            v_rows = [v_ref[0, u // 2, b0 + i, (u % 2) * V_DIM:(u % 2 + 1) * V_DIM, :] for i in range(nb)]
            v_ones = jnp.concatenate([jnp.concatenate(v_rows, axis=1), jnp.ones((SUM_ROWS, rows), BF16)], axis=0)
            pv = _dot(v_ones, p_ref[:rows]).reshape(v_tiles + SUM_ROWS // sub, sub, ng)
            acc_ref[u] = alpha * acc_ref[u] + pv

    def tile_group(b0, sizes):
        starts = [b0 + sum(sizes[:r]) for r in range(len(sizes))]
        for r, nb in enumerate(sizes):
            scores(starts[r], nb, r)
        for r, nb in enumerate(sizes):
            accumulate(starts[r], nb, r)

    tiles = [n_blocks % tile_blocks] * bool(n_blocks % tile_blocks) + [tile_blocks] * (n_blocks // tile_blocks)
    head = len(tiles) % nbuf
    if head == 0 and tiles[0] != tile_blocks:
        head = nbuf
    if head:
        tile_group(0, tiles[:head])
    first = sum(tiles[:head])
    n_loop = (len(tiles) - head) // nbuf
    if n_loop == 1:
        tile_group(first, tiles[head:])
    elif n_loop > 1:
        def body(i, carry):
            tile_group(first + nbuf * tile_blocks * i, [tile_blocks] * nbuf)
            return carry
        lax.fori_loop(0, n_loop, body, 0)

    for u in range(units):
        o = (acc_ref[u, :v_tiles] / acc_ref[u, v_tiles]).reshape(V_DIM, ng)
        if mla:
            o_ref[0, u * V_DIM:(u + 1) * V_DIM, :] = o.astype(BF16)
        else:
            for g in range(group):
                r0 = (u * group + g) * HEAD_DIM
                o_ref[0, r0:r0 + HEAD_DIM, :] = o[:, g * tq:(g + 1) * tq].astype(BF16)


def _attn_dense(qT, k, vT, sink_cols, *, mla, tq, pairs=1):
    b, _, t = qT.shape
    tk = KV_TILE
    n = k.shape[1]
    n_pairs = N_HEADS // 2 if mla else KV_HEADS // 2
    units = 2 * pairs
    q_rows = pairs * (2 * MLA_HEAD_PAD if mla else 2 * GROUP * HEAD_DIM)
    o_rows = pairs * (2 * V_DIM if mla else 2 * GROUP * HEAD_DIM)
    pair_cols = 2 * MLA_HEAD_PAD if mla else 2 * HEAD_DIM
    ng = tq if mla else GROUP * tq
    tile_blocks = min(TILE_BLOCKS, n // tk)
    nbuf = min(TILE_GROUP, pl.cdiv(n // tk, tile_blocks))
    tile_rows = tile_blocks * tk
    k_pm = jnp.swapaxes(k.reshape(b, n, n_pairs, pair_cols), 1, 2)
    v_pm = jnp.swapaxes(vT.reshape(b, n // tk, n_pairs, 2 * V_DIM, tk), 1, 2)
    in_specs = [pl.BlockSpec((1, q_rows, tq), lambda i, p, j: (i, p, j)),
                pl.BlockSpec((1, pairs, n, pair_cols), lambda i, p, j: (i, p, 0, 0)),
                pl.BlockSpec((1, pairs, n // tk, 2 * V_DIM, tk), lambda i, p, j: (i, p, 0, 0, 0))]
    args = [qT, k_pm, v_pm]
    if sink_cols is not None:
        in_specs.append(pl.BlockSpec((units, 1, ng), lambda i, p, j: (p, 0, 0)))
        args.append(sink_cols)
    sub = SUBLANES
    scratch = [pltpu.VMEM((tile_rows // sub, sub, ng), F32)] * (units * nbuf)
    scratch += [pltpu.VMEM((tile_rows, ng), BF16)] * (units * nbuf)
    if not mla:
        scratch.append(pltpu.VMEM((units, 2 * HEAD_DIM, ng), BF16))
    scratch += [pltpu.VMEM((units, (V_DIM + SUM_ROWS) // sub, sub, ng), F32),
                pltpu.VMEM((units, sub, ng), F32)]
    return pl.pallas_call(
        functools.partial(_attn_dense_kernel, mla=mla, units=units, tq=tq, n_blocks=n // tk,
                          has_sink=sink_cols is not None),
        grid=(b, n_pairs // pairs, t // tq), in_specs=in_specs,
        out_specs=pl.BlockSpec((1, o_rows, tq), lambda i, p, j: (i, p, j)),
        out_shape=jax.ShapeDtypeStruct((b, N_HEADS * V_DIM, t), BF16),
        scratch_shapes=scratch, compiler_params=_params(3),
        name="attn_mla" if mla else "attn_gqa",
    )(*args)


def _attn_window_kernel(q_ref, kc_ref, vc_ref, kp_ref, vp_ref, kcur_ref, vcur_ref, kn_ref, vn_ref,
                        sink_ref, o_ref, *scratch, tq, n_steps):
    wb = WINDOW_BLOCK
    step = pl.program_id(2)
    n_sub = tq // wb
    s_refs, p_refs = scratch[:2 * n_sub], scratch[2 * n_sub:]
    ng = GROUP * wb
    k_off = lax.broadcasted_iota(jnp.int32, (wb, ng), 0)
    q_off = lax.broadcasted_iota(jnp.int32, (wb, ng), 1) % wb
    keep_prev = k_off >= q_off
    keep_next = k_off <= q_off
    first = step == 0
    last = step == n_steps - 1
    n_ctx = kc_ref.shape[1]
    ones_rows = jnp.ones((SUM_ROWS, n_ctx + 3 * wb), BF16)

    def kv_block(i):
        if i < 0:
            return kp_ref[0], vp_ref[0, 0]
        if i >= n_sub:
            return kn_ref[0], vn_ref[0, 0]
        c, r = divmod(i * wb, KV_TILE)
        return kcur_ref[0, i * wb:(i + 1) * wb, :], vcur_ref[0, c, :, r:r + wb]

    sub = SUBLANES
    n_keys = n_ctx + 3 * wb
    v_tiles = V_DIM // sub

    def tiles(x):
        return x.reshape(x.shape[0] // sub, sub, x.shape[1])

    dz = jnp.minimum(step, 0)

    for i in range(n_sub):
        kcat = jnp.concatenate([kc_ref[0], kv_block(i - 1)[0], kv_block(i)[0], kv_block(i + 1)[0]], axis=0)
        keep0 = jnp.logical_and(keep_prev, jnp.logical_not(first)) if i == 0 else keep_prev
        keep2 = jnp.logical_and(keep_next, jnp.logical_not(last)) if i == n_sub - 1 else keep_next
        for u in range(2):
            s = _dot(kcat, _q_operand(q_ref, u, i * wb, wb, GROUP))
            s_ref = s_refs[2 * i + u]
            t0, t1, t2 = n_ctx // sub, (n_ctx + wb) // sub, (n_ctx + 2 * wb) // sub
            s_ref[dz, :t0] = tiles(s[:n_ctx])
            s_ref[dz, t0:t1] = tiles(jnp.where(keep0, s[n_ctx:n_ctx + wb], NEG_INF))
            s_ref[dz, t1:t2] = tiles(s[n_ctx + wb:n_ctx + 2 * wb])
            s_ref[dz, t2:] = tiles(jnp.where(keep2, s[n_ctx + 2 * wb:], NEG_INF))

    for i in range(n_sub):
        vcat = jnp.concatenate([vc_ref[0, 0], kv_block(i - 1)[1], kv_block(i)[1], kv_block(i + 1)[1]], axis=1)
        for u in range(2):
            s_ref, p_ref = s_refs[2 * i + u], p_refs[2 * i + u]
            sink = jnp.broadcast_to(sink_ref[u], (sub, ng))
            m = jnp.max(s_ref[dz], axis=0)
            for shift in (4, 2, 1):
                m = jnp.maximum(m, pltpu.roll(m, shift, 0))
            m = jnp.maximum(m, sink)
            p_ref[...] = jnp.exp2(s_ref[dz] - m).reshape(n_keys, ng).astype(BF16)
            v_ones = jnp.concatenate([vcat[u * V_DIM:(u + 1) * V_DIM, :], ones_rows], axis=0)
            r = tiles(_dot(v_ones, p_ref[...]))
            o = (r[:v_tiles] / (r[v_tiles] + jnp.exp2(sink - m))).reshape(V_DIM, ng)
            for g in range(GROUP):
                r0 = (u * GROUP + g) * HEAD_DIM
                o_ref[0, r0:r0 + HEAD_DIM, i * wb:(i + 1) * wb] = o[:, g * wb:(g + 1) * wb].astype(BF16)


def _attn_window(qT, k_ctx, vT_ctx, k, vT, sink_cols, *, tq):
    b, _, t = qT.shape
    wb = WINDOW_BLOCK
    n_ctx = k_ctx.shape[1]
    n_steps = t // tq
    n_sub = tq // wb
    n_wb = t // wb
    per_tile = KV_TILE // wb
    pair_rows = 2 * GROUP * HEAD_DIM
    ng = GROUP * wb

    def prev_blk(j):
        return jnp.maximum(j * n_sub - 1, 0)

    def next_blk(j):
        return jnp.minimum((j + 1) * n_sub, n_wb - 1)

    in_specs = [
        pl.BlockSpec((1, pair_rows, tq), lambda i, p, j: (i, p, j)),
        pl.BlockSpec((1, n_ctx, 2 * HEAD_DIM), lambda i, p, j: (i, 0, p)),
        pl.BlockSpec((1, n_ctx // KV_TILE, 2 * V_DIM, KV_TILE), lambda i, p, j: (i, 0, p, 0)),
        pl.BlockSpec((1, wb, 2 * HEAD_DIM), lambda i, p, j: (i, prev_blk(j), p)),
        pl.BlockSpec((1, 1, 2 * V_DIM, wb), lambda i, p, j: (i, prev_blk(j) // per_tile, p, prev_blk(j) % per_tile)),
        pl.BlockSpec((1, tq, 2 * HEAD_DIM), lambda i, p, j: (i, j, p)),
        pl.BlockSpec((1, tq // KV_TILE, 2 * V_DIM, KV_TILE), lambda i, p, j: (i, j, p, 0)),
        pl.BlockSpec((1, wb, 2 * HEAD_DIM), lambda i, p, j: (i, next_blk(j), p)),
        pl.BlockSpec((1, 1, 2 * V_DIM, wb), lambda i, p, j: (i, next_blk(j) // per_tile, p, next_blk(j) % per_tile)),
        pl.BlockSpec((2, 1, ng), lambda i, p, j: (p, 0, 0)),
    ]
    return pl.pallas_call(
        functools.partial(_attn_window_kernel, tq=tq, n_steps=n_steps),
        grid=(b, KV_HEADS // 2, n_steps), in_specs=in_specs,
        out_specs=pl.BlockSpec((1, pair_rows, tq), lambda i, p, j: (i, p, j)),
        out_shape=jax.ShapeDtypeStruct((b, N_HEADS * HEAD_DIM, t), BF16),
        scratch_shapes=([pltpu.VMEM((1, (n_ctx + 3 * wb) // SUBLANES, SUBLANES, ng), F32)] * (2 * n_sub)
                        + [pltpu.VMEM((n_ctx + 3 * wb, ng), BF16)] * (2 * n_sub)),
        compiler_params=_params(3), name="attn_window",
    )(qT, k_ctx, vT_ctx, k, vT, k, vT, k, vT, sink_cols)


def _post_kernel(*refs, final):
    x_ref, oT_ref, mod_ref, g2_ref, wo_ref, win_ref, wout_ref = refs[:7]
    pos = 7
    if final:
        gf_ref = refs[pos]
        pos += 1
    out_ref = refs[pos]
    mod = mod_ref[0]
    gate, shift2, scale2, gate2 = (mod[:, c * D_MODEL:(c + 1) * D_MODEL] for c in range(2, 6))
    attn = lax.dot_general(oT_ref[0], wo_ref[...], (((0,), (0,)), ((), ())), preferred_element_type=F32)
    x1 = x_ref[0] + gate * attn
    h2 = _norm_mod(x1, g2_ref[...], shift2, scale2).astype(BF16)
    acc = jnp.zeros_like(x1)
    for c in range(D_FF // FF_CHUNK):
        u = jnp.maximum(_dot(h2, win_ref[:, c * FF_CHUNK:(c + 1) * FF_CHUNK]), 0.0)
        acc = acc + _dot((u * u).astype(BF16), wout_ref[c * FF_CHUNK:(c + 1) * FF_CHUNK, :])
    x2 = x1 + gate2 * acc
    out_ref[0] = _rmsnorm(x2, gf_ref[...]) if final else x2


def _post(x, oT, mods, mod_row, g2, wo, win, wout, g_final):
    b, t, _ = x.shape
    tm = min(t, MLP_ROW_TILE)
    final = g_final is not None
    const = lambda i, j: (0, 0)
    in_specs = [pl.BlockSpec((1, tm, D_MODEL), lambda i, j: (i, j, 0)),
                pl.BlockSpec((1, D_MODEL, tm), lambda i, j: (i, 0, j)),
                pl.BlockSpec((1, 1, 6 * D_MODEL), lambda i, j: (mod_row(i), 0, 0)),
                pl.BlockSpec((1, D_MODEL), const),
                pl.BlockSpec((D_MODEL, D_MODEL), const, pipeline_mode=pl.Buffered(1)),
                pl.BlockSpec((D_MODEL, D_FF), const, pipeline_mode=pl.Buffered(1)),
                pl.BlockSpec((D_FF, D_MODEL), const, pipeline_mode=pl.Buffered(1))]
    args = [x, oT, mods, g2.reshape(1, D_MODEL), wo, win, wout]
    if final:
        in_specs.append(pl.BlockSpec((1, D_MODEL), const))
        args.append(g_final.reshape(1, D_MODEL))
    return pl.pallas_call(
        functools.partial(_post_kernel, final=final),
        grid=(b, t // tm), in_specs=in_specs,
        out_specs=pl.BlockSpec((1, tm, D_MODEL), lambda i, j: (i, j, 0)),
        out_shape=jax.ShapeDtypeStruct((b, t, D_MODEL), F32),
        compiler_params=_params(2), name="post_mlp",
    )(*args)


def _sink_cols(sink, width):
    return jnp.repeat((sink.astype(F32) * LOG2_E).reshape(KV_HEADS, 1, GROUP), width, axis=2)


def _mla_weights(w_uq, w_dkv, w_ukv):
    wuq_pad = jnp.pad(w_uq.reshape(Q_LORA, N_HEADS, NOPE_DIM + ROPE_DIM),
                      ((0, 0), (0, 0), (0, MLA_HEAD_PAD - NOPE_DIM - ROPE_DIM)))
    wuq_pad = wuq_pad.reshape(Q_LORA, N_HEADS * MLA_HEAD_PAD).astype(BF16)
    wdkv_pad = jnp.pad(w_dkv, ((0, 0), (0, CKR_PAD - KV_LORA - ROPE_DIM))).astype(BF16)
    w3 = w_ukv.reshape(KV_LORA, N_HEADS, NOPE_DIM + V_DIM)
    wk_nope = jnp.pad(w3[:, :, :NOPE_DIM], ((0, 0), (0, 0), (0, MLA_HEAD_PAD - NOPE_DIM)))
    place = jnp.pad(jnp.eye(ROPE_DIM, dtype=F32), ((0, 0), (NOPE_DIM, MLA_HEAD_PAD - NOPE_DIM - ROPE_DIM)))
    place = jnp.broadcast_to(place[:, None, :], (ROPE_DIM, N_HEADS, MLA_HEAD_PAD))
    wk_exp = jnp.concatenate([wk_nope.reshape(KV_LORA, -1), place.reshape(ROPE_DIM, -1),
                              jnp.zeros((CKR_PAD - KV_LORA - ROPE_DIM, N_HEADS * MLA_HEAD_PAD), F32)], axis=0)
    wv = w3[:, :, NOPE_DIM:].reshape(KV_LORA, N_HEADS * V_DIM)
    return wuq_pad, wdkv_pad, wk_exp.astype(BF16), wv.astype(BF16)


def _ctx_gqa(cache_k, cache_v):
    b, p = cache_k.shape[:2]
    k = cache_k.reshape(b, p, KV_HEADS * HEAD_DIM).astype(BF16)
    v = cache_v.reshape(b, p // KV_TILE, KV_TILE, KV_HEADS * HEAD_DIM)
    return k, jnp.swapaxes(v, 2, 3).astype(BF16)


def kernel(x_prompt, x_sample, c, cache_a_k, cache_a_v, cache_b_ckv, cache_b_krope, cache_c_k, cache_c_v, c_ctx, w_ada, b_ada, norm_g, w_mlp_in, w_mlp_out, a_w_qkv, a_sink, a_w_o, b_w_dq, b_g_q, b_w_uq, b_w_dkv, b_g_kv, b_w_ukv, b_w_o, c_w_qkv, c_g_q, c_g_k, c_w_o, g_final):
    n_batch, seq = x_prompt.shape[:2]
    n_dec, dec_seq = x_sample.shape[:2]
    cond = jnp.concatenate([c_ctx[None, :], c, jnp.zeros((8 - 1 - n_dec, D_MODEL), F32)], axis=0)
    mods_all = _ada_all(cond, w_ada, b_ada)
    tab64 = _rope_tables(dec_seq, HEAD_DIM)
    tab32 = _rope_tables(dec_seq, ROPE_DIM)
    prompt_row = lambda i: 0
    sample_row = lambda i: i + 1

    xp, xs = x_prompt, x_sample
    st = {"a_k": [], "a_v": [], "b_ckv": [], "b_kr": [], "c_k": [], "c_v": []}
    for layer in range(DEPTH):
        kind, j = layer % 3, layer // 3
        mods = mods_all[layer].reshape(8, 1, 6 * D_MODEL)
        g1, g2 = norm_g[layer, 0], norm_g[layer, 1]
        if kind == 0:
            sink = a_sink[j]
            qT, k, vT, ks, vs = _proj_gqa(xp, mods, prompt_row, g1, a_w_qkv[j], None, None, None, states=True)
            st["a_k"].append(ks)
            st["a_v"].append(vs)
            oT_p = _attn_dense(qT, k, vT, _sink_cols(sink, seq), mla=False, tq=seq, pairs=KV_HEADS // 2)
            qT, k, vT = _proj_gqa(xs, mods, sample_row, g1, a_w_qkv[j], tab64, None, None, states=False)
            k_ctx, vT_ctx = _ctx_gqa(cache_a_k[:, j], cache_a_v[:, j])
            oT_s = _attn_window(qT, k_ctx, vT_ctx, k, vT, _sink_cols(sink, WINDOW_BLOCK), tq=512)
            wo = a_w_o[j]
        elif kind == 1:
            wuq_pad, wdkv_pad, wk_exp, wv = _mla_weights(b_w_uq[j], b_w_dkv[j], b_w_ukv[j])
            qT, ck, cs, krs = _proj_mla(xp, mods, prompt_row, g1, b_w_dq[j], b_g_q[j], wuq_pad, wdkv_pad,
                                        b_g_kv[j], None, states=True)
            st["b_ckv"].append(cs)
            st["b_kr"].append(krs)
            oT_p = _attn_dense(qT, *_expand_mla(ck, wk_exp, wv), None, mla=True, tq=seq, pairs=N_HEADS // 2)
            qT, ck = _proj_mla(xs, mods, sample_row, g1, b_w_dq[j], b_g_q[j], wuq_pad, wdkv_pad,
                               b_g_kv[j], tab32, states=False)
            ck_ctx = jnp.concatenate(
                [cache_b_ckv[:, j], cache_b_krope[:, j],
                 jnp.zeros(cache_b_krope.shape[:1] + cache_b_krope.shape[2:3] + (CKR_PAD - KV_LORA - ROPE_DIM,), F32)],
                axis=-1).astype(BF16)
            ck_all = jnp.concatenate([ck_ctx, ck], axis=1)
            oT_s = _attn_dense(qT, *_expand_mla(ck_all, wk_exp, wv), None, mla=True, tq=1024)
            wo = b_w_o[j]
        else:
            qT, k, vT, ks, vs = _proj_gqa(xp, mods, prompt_row, g1, c_w_qkv[j], None, c_g_q[j], c_g_k[j],
                                          states=True)
            st["c_k"].append(ks)
            st["c_v"].append(vs)
            oT_p = _attn_dense(qT, k, vT, None, mla=False, tq=seq, pairs=KV_HEADS // 2)
            qT, k, vT = _proj_gqa(xs, mods, sample_row, g1, c_w_qkv[j], tab64, c_g_q[j], c_g_k[j], states=False)
            k_ctx, vT_ctx = _ctx_gqa(cache_c_k[:, j], cache_c_v[:, j])
            oT_s = _attn_dense(qT, jnp.concatenate([k_ctx, k], axis=1), jnp.concatenate([vT_ctx, vT], axis=1),
                               None, mla=False, tq=512)
            wo = c_w_o[j]
        gf = g_final if layer == DEPTH - 1 else None
        wo, win, wout = wo.astype(BF16), w_mlp_in[layer].astype(BF16), w_mlp_out[layer].astype(BF16)
        xp = _post(xp, oT_p, mods, prompt_row, g2, wo, win, wout, gf)
        xs = _post(xs, oT_s, mods, sample_row, g2, wo, win, wout, gf)

    def stack_heads(parts):
        return jnp.stack([p.reshape(n_batch, seq, KV_HEADS, HEAD_DIM) for p in parts], axis=1)

    return (xp, xs, stack_heads(st["a_k"]), stack_heads(st["a_v"]),
            jnp.stack(st["b_ckv"], axis=1), jnp.stack(st["b_kr"], axis=1),
            stack_heads(st["c_k"]), stack_heads(st["c_v"]))
```

```python
import functools

import jax
import jax.numpy as jnp
from jax import lax
from jax.experimental import pallas as pl
from jax.experimental.pallas import tpu as pltpu

D_MODEL = 1024
DEPTH = 4
N_HEADS = 16
KV_HEADS = 4
GROUP = N_HEADS // KV_HEADS
HEAD_DIM = 64
QKV_DIM = (N_HEADS + 2 * KV_HEADS) * HEAD_DIM
Q_LORA = 384
KV_LORA = 256
NOPE_DIM = 64
ROPE_DIM = 32
V_DIM = 64
D_FF = 4 * D_MODEL
GRID_W = 64
WINDOW_BLOCK = 128
ROPE_THETA = 10000.0
EPS = 1e-6
NEG_INF = -1e30
LOG2_E = 1.4426950408889634
ATTN_Q_SCALE = HEAD_DIM ** -0.5 * LOG2_E
MLA_Q_SCALE = (NOPE_DIM + ROPE_DIM) ** -0.5 * LOG2_E

MLA_HEAD_PAD = 128
CKR_PAD = KV_LORA + 128
ROW_TILE = 256
PROJ_STEP_ROWS = 512
MLP_ROW_TILE = 512
KV_TILE = 256
SOFTMAX_CHUNK = 256
SUBLANES = 8
TILE_BLOCKS = 1
TILE_GROUP = 4
SUM_ROWS = 16
FF_CHUNK = 1024
V7X_VMEM_LIMIT_BYTES = 48 * 1024 * 1024

F32 = jnp.float32
BF16 = jnp.bfloat16


def _params(n_axes):
    return pltpu.CompilerParams(dimension_semantics=("arbitrary",) * n_axes,
                                vmem_limit_bytes=V7X_VMEM_LIMIT_BYTES)


def _dot(a, b):
    return jnp.dot(a, b, preferred_element_type=F32)


def _rmsnorm(x, g):
    ms = jnp.mean(x * x, axis=-1, keepdims=True)
    return x * lax.rsqrt(ms + EPS) * g


def _norm_mod(x, g, shift, scale):
    return _rmsnorm(x, g) * (1.0 + scale) + shift


def _ada_kernel(cond_ref, w_ref, b_ref, o_ref):
    cnd = cond_ref[...]
    act = cnd / (1.0 + jnp.exp(-cnd))
    o_ref[0] = _dot(act.astype(BF16), w_ref[0].astype(BF16)) + b_ref[0]


def _ada_all(cond, w_ada, b_ada):
    tn = 1536
    n = 6 * D_MODEL
    return pl.pallas_call(
        _ada_kernel,
        grid=(DEPTH, n // tn),
        in_specs=[pl.BlockSpec((8, D_MODEL), lambda l, j: (0, 0)),
                  pl.BlockSpec((1, D_MODEL, tn), lambda l, j: (l, 0, j)),
                  pl.BlockSpec((1, 1, tn), lambda l, j: (l, 0, j))],
        out_specs=pl.BlockSpec((1, 8, tn), lambda l, j: (l, 0, j)),
        out_shape=jax.ShapeDtypeStruct((DEPTH, 8, n), F32),
        compiler_params=_params(2),
        name="ada_mod",
    )(cond, w_ada, b_ada.reshape(DEPTH, 1, n))


def _rope_t(x3, tab, half):
    a, b = x3[:, 0:half], x3[:, half:2 * half]
    c, d = x3[:, 2 * half:3 * half], x3[:, 3 * half:4 * half]
    cr, sr, cc, sc = tab[0], tab[1], tab[2], tab[3]
    return jnp.concatenate([a * cr - b * sr, a * sr + b * cr, c * cc - d * sc, c * sc + d * cc], axis=1)


def _rope_tables(t_len, dim):
    half = dim // 2
    pos = jnp.arange(t_len)
    rows = (pos // GRID_W).astype(F32)
    cols = (pos % GRID_W).astype(F32)
    freqs = ROPE_THETA ** (-jnp.arange(0, half, 2, dtype=F32) / half)
    ang_r = freqs[:, None] * rows[None, :]
    ang_c = freqs[:, None] * cols[None, :]
    return jnp.stack([jnp.cos(ang_r), jnp.sin(ang_r), jnp.cos(ang_c), jnp.sin(ang_c)])


def _proj_gqa_kernel(*refs, rope, qknorm, states):
    x_ref, mod_ref, g_ref, w_ref = refs[:4]
    pos = 4
    if rope:
        tab_ref = refs[pos]
        pos += 1
    if qknorm:
        gq_ref, gk_ref = refs[pos], refs[pos + 1]
        pos += 2
    qT_ref, k_ref, vT_ref = refs[pos:pos + 3]
    pos += 3
    tm = ROW_TILE
    nq = N_HEADS * HEAD_DIM
    nk = KV_HEADS * HEAD_DIM

    mod = mod_ref[0]
    for sub_tile in range(x_ref.shape[1] // tm):
        rows = slice(sub_tile * tm, (sub_tile + 1) * tm)
        h = _norm_mod(x_ref[0, rows], g_ref[...], mod[:, 0:D_MODEL], mod[:, D_MODEL:2 * D_MODEL])
        y = _dot(h.astype(BF16), w_ref[...])
        q3 = y[:, :nq].T.reshape(N_HEADS, HEAD_DIM, tm)
        k3 = y[:, nq:nq + nk].T.reshape(KV_HEADS, HEAD_DIM, tm)
        v = y[:, nq + nk:]
        if qknorm:
            q3 = q3 * lax.rsqrt(jnp.mean(q3 * q3, axis=1, keepdims=True) + EPS) * gq_ref[...]
            k3 = k3 * lax.rsqrt(jnp.mean(k3 * k3, axis=1, keepdims=True) + EPS) * gk_ref[...]
        if states:
            ks_ref, vs_ref = refs[pos], refs[pos + 1]
            ks_ref[0, rows] = k3.reshape(nk, tm).T if qknorm else y[:, nq:nq + nk]
            vs_ref[0, rows] = v
        if rope:
            tab = tab_ref[:, :, rows]
            q3 = _rope_t(q3, tab, HEAD_DIM // 4)
            k3 = _rope_t(k3, tab, HEAD_DIM // 4)
        qT_ref[0, :, rows] = (q3 * ATTN_Q_SCALE).reshape(nq, tm).astype(BF16)
        k_ref[0, rows] = k3.reshape(nk, tm).T.astype(BF16)
        vT_ref[0, sub_tile] = v.T.astype(BF16)


def _proj_gqa(x, mods, mod_row, g, w, tab, gq, gk, *, states):
    b, t, _ = x.shape
    tm = min(t, PROJ_STEP_ROWS)
    nq = N_HEADS * HEAD_DIM
    nk = KV_HEADS * HEAD_DIM
    rope = tab is not None
    qknorm = gq is not None
    in_specs = [pl.BlockSpec((1, tm, D_MODEL), lambda i, j: (i, j, 0)),
                pl.BlockSpec((1, 1, 6 * D_MODEL), lambda i, j: (mod_row(i), 0, 0)),
                pl.BlockSpec((1, D_MODEL), lambda i, j: (0, 0)),
                pl.BlockSpec((D_MODEL, QKV_DIM), lambda i, j: (0, 0))]
    args = [x, mods, g.reshape(1, D_MODEL), w.astype(BF16)]
    if rope:
        in_specs.append(pl.BlockSpec((4, HEAD_DIM // 4, tm), lambda i, j: (0, 0, j)))
        args.append(tab)
    if qknorm:
        in_specs += [pl.BlockSpec((HEAD_DIM, 1), lambda i, j: (0, 0))] * 2
        args += [gq.reshape(HEAD_DIM, 1), gk.reshape(HEAD_DIM, 1)]
    out_specs = [pl.BlockSpec((1, nq, tm), lambda i, j: (i, 0, j)),
                 pl.BlockSpec((1, tm, nk), lambda i, j: (i, j, 0)),
                 pl.BlockSpec((1, tm // ROW_TILE, nk, ROW_TILE), lambda i, j: (i, j, 0, 0))]
    out_shape = [jax.ShapeDtypeStruct((b, nq, t), BF16),
                 jax.ShapeDtypeStruct((b, t, nk), BF16),
                 jax.ShapeDtypeStruct((b, t // ROW_TILE, nk, ROW_TILE), BF16)]
    if states:
        out_specs += [pl.BlockSpec((1, tm, nk), lambda i, j: (i, j, 0))] * 2
        out_shape += [jax.ShapeDtypeStruct((b, t, nk), F32)] * 2
    return pl.pallas_call(
        functools.partial(_proj_gqa_kernel, rope=rope, qknorm=qknorm, states=states),
        grid=(b, t // tm), in_specs=in_specs, out_specs=out_specs, out_shape=out_shape,
        compiler_params=_params(2), name="proj_gqa",
    )(*args)


def _proj_mla_kernel(*refs, rope, states):
    x_ref, mod_ref, g_ref, wdq_ref, gq_ref, wuq_ref, wdkv_ref, gkv_ref = refs[:8]
    pos = 8
    if rope:
        tab_ref = refs[pos]
        pos += 1
    qT_ref, ck_ref = refs[pos], refs[pos + 1]
    pos += 2
    tm = ROW_TILE
    quarter = ROPE_DIM // 4

    mod = mod_ref[0]
    for sub_tile in range(x_ref.shape[1] // tm):
        rows = slice(sub_tile * tm, (sub_tile + 1) * tm)
        h = _norm_mod(x_ref[0, rows], g_ref[...], mod[:, 0:D_MODEL], mod[:, D_MODEL:2 * D_MODEL]).astype(BF16)
        qn = _rmsnorm(_dot(h, wdq_ref[...]), gq_ref[...])
        q3 = _dot(qn.astype(BF16), wuq_ref[...]).T.reshape(N_HEADS, MLA_HEAD_PAD, tm)
        ckv = _dot(h, wdkv_ref[...])
        cn = _rmsnorm(ckv[:, :KV_LORA], gkv_ref[...])
        kr = ckv[:, KV_LORA:]
        if states:
            cs_ref, krs_ref = refs[pos], refs[pos + 1]
            cs_ref[0, rows] = cn
            krs_ref[0, rows] = kr[:, :ROPE_DIM]
        if rope:
            tab = tab_ref[:, :, rows]
            q_rot = _rope_t(q3[:, NOPE_DIM:NOPE_DIM + ROPE_DIM], tab, quarter)
            q3 = jnp.concatenate([q3[:, :NOPE_DIM], q_rot, q3[:, NOPE_DIM + ROPE_DIM:]], axis=1)
            krT = kr.T
            kr_rot = _rope_t(krT[:ROPE_DIM].reshape(1, ROPE_DIM, tm), tab, quarter).reshape(ROPE_DIM, tm)
            kr = jnp.concatenate([kr_rot, krT[ROPE_DIM:]], axis=0).T
        qT_ref[0, :, rows] = (q3 * MLA_Q_SCALE).reshape(N_HEADS * MLA_HEAD_PAD, tm).astype(BF16)
        ck_ref[0, rows] = jnp.concatenate([cn, kr], axis=1).astype(BF16)


def _proj_mla(x, mods, mod_row, g, wdq, gq, wuq_pad, wdkv_pad, gkv, tab, *, states):
    b, t, _ = x.shape
    tm = min(t, PROJ_STEP_ROWS)
    rope = tab is not None
    nqp = N_HEADS * MLA_HEAD_PAD
    const = lambda i, j: (0, 0)
    in_specs = [pl.BlockSpec((1, tm, D_MODEL), lambda i, j: (i, j, 0)),
                pl.BlockSpec((1, 1, 6 * D_MODEL), lambda i, j: (mod_row(i), 0, 0)),
                pl.BlockSpec((1, D_MODEL), const),
                pl.BlockSpec((D_MODEL, Q_LORA), const),
                pl.BlockSpec((1, Q_LORA), const),
                pl.BlockSpec((Q_LORA, nqp), const),
                pl.BlockSpec((D_MODEL, CKR_PAD), const),
                pl.BlockSpec((1, KV_LORA), const)]
    args = [x, mods, g.reshape(1, D_MODEL), wdq.astype(BF16), gq.reshape(1, Q_LORA), wuq_pad,
            wdkv_pad, gkv.reshape(1, KV_LORA)]
    if rope:
        in_specs.append(pl.BlockSpec((4, ROPE_DIM // 4, tm), lambda i, j: (0, 0, j)))
        args.append(tab)
    out_specs = [pl.BlockSpec((1, nqp, tm), lambda i, j: (i, 0, j)),
                 pl.BlockSpec((1, tm, CKR_PAD), lambda i, j: (i, j, 0))]
    out_shape = [jax.ShapeDtypeStruct((b, nqp, t), BF16),
                 jax.ShapeDtypeStruct((b, t, CKR_PAD), BF16)]
    if states:
        out_specs += [pl.BlockSpec((1, tm, KV_LORA), lambda i, j: (i, j, 0)),
                      pl.BlockSpec((1, tm, ROPE_DIM), lambda i, j: (i, j, 0))]
        out_shape += [jax.ShapeDtypeStruct((b, t, KV_LORA), F32),
                      jax.ShapeDtypeStruct((b, t, ROPE_DIM), F32)]
    return pl.pallas_call(
        functools.partial(_proj_mla_kernel, rope=rope, states=states),
        grid=(b, t // tm), in_specs=in_specs, out_specs=out_specs, out_shape=out_shape,
        compiler_params=_params(2), name="proj_mla",
    )(*args)


def _expand_kernel(ck_ref, wk_ref, wv_ref, k_ref, vT_ref):
    ck = ck_ref[0]
    k = _dot(ck, wk_ref[...]).astype(BF16)
    vT = _dot(ck[:, :KV_LORA], wv_ref[...]).T.astype(BF16)
    for p in range(N_HEADS // 2):
        k_ref[0, p] = k[:, p * 2 * MLA_HEAD_PAD:(p + 1) * 2 * MLA_HEAD_PAD]
        vT_ref[0, p, 0] = vT[p * 2 * V_DIM:(p + 1) * 2 * V_DIM, :]


def _expand_mla(ck, wk_exp, wv):
    b, t, _ = ck.shape
    tm = ROW_TILE
    nkp = N_HEADS * MLA_HEAD_PAD
    nv = N_HEADS * V_DIM
    n_pairs = N_HEADS // 2
    return pl.pallas_call(
        _expand_kernel,
        grid=(b, t // tm),
        in_specs=[pl.BlockSpec((1, tm, CKR_PAD), lambda i, j: (i, j, 0)),
                  pl.BlockSpec((CKR_PAD, nkp), lambda i, j: (0, 0)),
                  pl.BlockSpec((KV_LORA, nv), lambda i, j: (0, 0))],
        out_specs=[pl.BlockSpec((1, n_pairs, tm, 2 * MLA_HEAD_PAD), lambda i, j: (i, 0, j, 0)),
                   pl.BlockSpec((1, n_pairs, 1, 2 * V_DIM, tm), lambda i, j: (i, 0, j, 0, 0))],
        out_shape=[jax.ShapeDtypeStruct((b, n_pairs, t, 2 * MLA_HEAD_PAD), BF16),
                   jax.ShapeDtypeStruct((b, n_pairs, t // tm, 2 * V_DIM, tm), BF16)],
        compiler_params=_params(2), name="expand_mla",
    )(ck, wk_exp, wv)


def _q_operand(q_ref, u, lo, width, group):
    qg = jnp.concatenate(
        [q_ref[0, (u * group + g) * HEAD_DIM:(u * group + g + 1) * HEAD_DIM, lo:lo + width] for g in range(group)],
        axis=1)
    z = jnp.zeros_like(qg)
    return jnp.concatenate([qg, z] if u % 2 == 0 else [z, qg], axis=0)


def _attn_dense_kernel(*refs, mla, units, tq, n_blocks, has_sink):
    q_ref, k_ref, v_ref = refs[:3]
    pos = 3
    sink_ref = None
    if has_sink:
        sink_ref = refs[pos]
        pos += 1
    o_ref = refs[pos]
    pos += 1
    tile_blocks = min(TILE_BLOCKS, n_blocks)
    nbuf = min(TILE_GROUP, pl.cdiv(n_blocks, tile_blocks))
    s_refs = [refs[pos + u * nbuf:pos + (u + 1) * nbuf] for u in range(units)]
    pos += units * nbuf
    p_refs = [refs[pos + u * nbuf:pos + (u + 1) * nbuf] for u in range(units)]
    pos += units * nbuf
    if mla:
        acc_ref, m_ref = refs[pos:]
        group = 1
    else:
        qp_ref, acc_ref, m_ref = refs[pos:]
        group = GROUP
        for u in range(units):
            qp_ref[u] = _q_operand(q_ref, u, 0, tq, group)
    ng = group * tq
    tk = KV_TILE
    cw = SOFTMAX_CHUNK
    sub = SUBLANES
    v_tiles = V_DIM // sub

    for u in range(units):
        m_ref[u] = jnp.broadcast_to(sink_ref[u], (sub, ng)) if has_sink else jnp.full((sub, ng), NEG_INF, F32)
        acc_ref[u, :v_tiles] = jnp.zeros((v_tiles, sub, ng), F32)
        acc_ref[u, v_tiles:] = jnp.full((SUM_ROWS // sub, sub, ng), 1.0 if has_sink else 0.0, F32)

    def scores(b0, nb, par):
        rows = nb * tk
        off = b0 * tk if isinstance(b0, int) else pl.multiple_of(b0 * tk, tk)
        for u in range(units):
            if mla:
                kp = k_ref[0, u // 2, pl.ds(off, rows), (u % 2) * MLA_HEAD_PAD:(u % 2 + 1) * MLA_HEAD_PAD]
                qop = q_ref[0, u * MLA_HEAD_PAD:(u + 1) * MLA_HEAD_PAD, :]
            else:
                kp = k_ref[0, u // 2, pl.ds(off, rows), :]
                qop = qp_ref[u]
            s_refs[u][par][:rows // sub] = _dot(kp, qop).reshape(rows // sub, sub, ng)

    def accumulate(b0, nb, par):
        rows = nb * tk
        for u in range(units):
            s_ref, p_ref = s_refs[u][par], p_refs[u][par]
            m_blk = jnp.max(s_ref[:rows // sub], axis=0)
            for shift in (4, 2, 1):
                m_blk = jnp.maximum(m_blk, pltpu.roll(m_blk, shift, 0))
            m_prev = m_ref[u]
            m_new = jnp.maximum(m_prev, m_blk)
            m_ref[u] = m_new
            alpha = jnp.exp2(m_prev - m_new)
            for c in range(ng // cw):
                cols = slice(c * cw, (c + 1) * cw)
                p = jnp.exp2(s_ref[:rows // sub, :, cols] - m_ref[u, :, cols])
                p_ref[:rows, cols] = p.reshape(rows, cw).astype(BF16)
            v_rows = [v_ref[0, u // 2, b0 + i, (u % 2) * V_DIM:(u % 2 + 1) * V_DIM, :] for i in range(nb)]
            v_ones = jnp.concatenate([jnp.concatenate(v_rows, axis=1), jnp.ones((SUM_ROWS, rows), BF16)], axis=0)
            pv = _dot(v_ones, p_ref[:rows]).reshape(v_tiles + SUM_ROWS // sub, sub, ng)
            acc_ref[u] = alpha * acc_ref[u] + pv

    def tile_group(b0, sizes):
        starts = [b0 + sum(sizes[:r]) for r in range(len(sizes))]
        for r, nb in enumerate(sizes):
            scores(starts[r], nb, r)
        for r, nb in enumerate(sizes):
            accumulate(starts[r], nb, r)

    tiles = [n_blocks % tile_blocks] * bool(n_blocks % tile_blocks) + [tile_blocks] * (n_blocks // tile_blocks)
    head = len(tiles) % nbuf
    if head == 0 and tiles[0] != tile_blocks:
        head = nbuf
    if head:
        tile_group(0, tiles[:head])
    first = sum(tiles[:head])
    n_loop = (len(tiles) - head) // nbuf
    if n_loop == 1:
        tile_group(first, tiles[head:])
    elif n_loop > 1:
        def body(i, carry):
            tile_group(first + nbuf * tile_blocks * i, [tile_blocks] * nbuf)
            return carry
        lax.fori_loop(0, n_loop, body, 0)

    for u in range(units):
        o = (acc_ref[u, :v_tiles] / acc_ref[u, v_tiles]).reshape(V_DIM, ng)
        if mla:
            o_ref[0, u * V_DIM:(u + 1) * V_DIM, :] = o.astype(BF16)
        else:
            for g in range(group):
                r0 = (u * group + g) * HEAD_DIM
                o_ref[0, r0:r0 + HEAD_DIM, :] = o[:, g * tq:(g + 1) * tq].astype(BF16)


def _attn_dense(qT, k, vT, sink_cols, *, mla, tq, pairs=1, pair_major=False):
    b, _, t = qT.shape
    tk = KV_TILE
    n = k.shape[2] if pair_major else k.shape[1]
    n_pairs = N_HEADS // 2 if mla else KV_HEADS // 2
    units = 2 * pairs
    q_rows = pairs * (2 * MLA_HEAD_PAD if mla else 2 * GROUP * HEAD_DIM)
    o_rows = pairs * (2 * V_DIM if mla else 2 * GROUP * HEAD_DIM)
    pair_cols = 2 * MLA_HEAD_PAD if mla else 2 * HEAD_DIM
    ng = tq if mla else GROUP * tq
    tile_blocks = min(TILE_BLOCKS, n // tk)
    nbuf = min(TILE_GROUP, pl.cdiv(n // tk, tile_blocks))
    tile_rows = tile_blocks * tk
    k_pm = k if pair_major else jnp.swapaxes(k.reshape(b, n, n_pairs, pair_cols), 1, 2)
    v_pm = vT if pair_major else jnp.swapaxes(vT.reshape(b, n // tk, n_pairs, 2 * V_DIM, tk), 1, 2)
    in_specs = [pl.BlockSpec((1, q_rows, tq), lambda i, p, j: (i, p, j)),
                pl.BlockSpec((1, pairs, n, pair_cols), lambda i, p, j: (i, p, 0, 0)),
                pl.BlockSpec((1, pairs, n // tk, 2 * V_DIM, tk), lambda i, p, j: (i, p, 0, 0, 0))]
    args = [qT, k_pm, v_pm]
    if sink_cols is not None:
        in_specs.append(pl.BlockSpec((units, 1, ng), lambda i, p, j: (p, 0, 0)))
        args.append(sink_cols)
    sub = SUBLANES
    scratch = [pltpu.VMEM((tile_rows // sub, sub, ng), F32)] * (units * nbuf)
    scratch += [pltpu.VMEM((tile_rows, ng), BF16)] * (units * nbuf)
    if not mla:
        scratch.append(pltpu.VMEM((units, 2 * HEAD_DIM, ng), BF16))
    scratch += [pltpu.VMEM((units, (V_DIM + SUM_ROWS) // sub, sub, ng), F32),
                pltpu.VMEM((units, sub, ng), F32)]
    return pl.pallas_call(
        functools.partial(_attn_dense_kernel, mla=mla, units=units, tq=tq, n_blocks=n // tk,
                          has_sink=sink_cols is not None),
        grid=(b, n_pairs // pairs, t // tq), in_specs=in_specs,
        out_specs=pl.BlockSpec((1, o_rows, tq), lambda i, p, j: (i, p, j)),
        out_shape=jax.ShapeDtypeStruct((b, N_HEADS * V_DIM, t), BF16),
        scratch_shapes=scratch, compiler_params=_params(3),
        name="attn_mla" if mla else "attn_gqa",
    )(*args)


def _attn_window_kernel(q_ref, kc_ref, vc_ref, kp_ref, vp_ref, kcur_ref, vcur_ref, kn_ref, vn_ref,
                        sink_ref, o_ref, *scratch, tq, n_steps):
    wb = WINDOW_BLOCK
    step = pl.program_id(2)
    n_sub = tq // wb
    s_refs, p_refs = scratch[:2 * n_sub], scratch[2 * n_sub:]
    ng = GROUP * wb
    k_off = lax.broadcasted_iota(jnp.int32, (wb, ng), 0)
    q_off = lax.broadcasted_iota(jnp.int32, (wb, ng), 1) % wb
    keep_prev = k_off >= q_off
    keep_next = k_off <= q_off
    first = step == 0
    last = step == n_steps - 1
    n_ctx = kc_ref.shape[1]
    ones_rows = jnp.ones((SUM_ROWS, n_ctx + 3 * wb), BF16)

    def kv_block(i):
        if i < 0:
            return kp_ref[0], vp_ref[0, 0]
        if i >= n_sub:
            return kn_ref[0], vn_ref[0, 0]
        c, r = divmod(i * wb, KV_TILE)
        return kcur_ref[0, i * wb:(i + 1) * wb, :], vcur_ref[0, c, :, r:r + wb]

    sub = SUBLANES
    n_keys = n_ctx + 3 * wb
    v_tiles = V_DIM // sub

    def tiles(x):
        return x.reshape(x.shape[0] // sub, sub, x.shape[1])

    dz = jnp.minimum(step, 0)

    for i in range(n_sub):
        kcat = jnp.concatenate([kc_ref[0], kv_block(i - 1)[0], kv_block(i)[0], kv_block(i + 1)[0]], axis=0)
        keep0 = jnp.logical_and(keep_prev, jnp.logical_not(first)) if i == 0 else keep_prev
        keep2 = jnp.logical_and(keep_next, jnp.logical_not(last)) if i == n_sub - 1 else keep_next
        for u in range(2):
            s = _dot(kcat, _q_operand(q_ref, u, i * wb, wb, GROUP))
            s_ref = s_refs[2 * i + u]
            t0, t1, t2 = n_ctx // sub, (n_ctx + wb) // sub, (n_ctx + 2 * wb) // sub
            s_ref[dz, :t0] = tiles(s[:n_ctx])
            s_ref[dz, t0:t1] = tiles(jnp.where(keep0, s[n_ctx:n_ctx + wb], NEG_INF))
            s_ref[dz, t1:t2] = tiles(s[n_ctx + wb:n_ctx + 2 * wb])
            s_ref[dz, t2:] = tiles(jnp.where(keep2, s[n_ctx + 2 * wb:], NEG_INF))

    for i in range(n_sub):
        vcat = jnp.concatenate([vc_ref[0, 0], kv_block(i - 1)[1], kv_block(i)[1], kv_block(i + 1)[1]], axis=1)
        for u in range(2):
            s_ref, p_ref = s_refs[2 * i + u], p_refs[2 * i + u]
            sink = jnp.broadcast_to(sink_ref[u], (sub, ng))
            m = jnp.max(s_ref[dz], axis=0)
            for shift in (4, 2, 1):
                m = jnp.maximum(m, pltpu.roll(m, shift, 0))
            m = jnp.maximum(m, sink)
            p_ref[...] = jnp.exp2(s_ref[dz] - m).reshape(n_keys, ng).astype(BF16)
            v_ones = jnp.concatenate([vcat[u * V_DIM:(u + 1) * V_DIM, :], ones_rows], axis=0)
            r = tiles(_dot(v_ones, p_ref[...]))
            o = (r[:v_tiles] / (r[v_tiles] + jnp.exp2(sink - m))).reshape(V_DIM, ng)
            for g in range(GROUP):
                r0 = (u * GROUP + g) * HEAD_DIM
                o_ref[0, r0:r0 + HEAD_DIM, i * wb:(i + 1) * wb] = o[:, g * wb:(g + 1) * wb].astype(BF16)


def _attn_window(qT, k_ctx, vT_ctx, k, vT, sink_cols, *, tq):
    b, _, t = qT.shape
    wb = WINDOW_BLOCK
    n_ctx = k_ctx.shape[1]
    n_steps = t // tq
    n_sub = tq // wb
    n_wb = t // wb
    per_tile = KV_TILE // wb
    pair_rows = 2 * GROUP * HEAD_DIM
    ng = GROUP * wb

    def prev_blk(j):
        return jnp.maximum(j * n_sub - 1, 0)

    def next_blk(j):
        return jnp.minimum((j + 1) * n_sub, n_wb - 1)

    in_specs = [
        pl.BlockSpec((1, pair_rows, tq), lambda i, p, j: (i, p, j)),
        pl.BlockSpec((1, n_ctx, 2 * HEAD_DIM), lambda i, p, j: (i, 0, p)),
        pl.BlockSpec((1, n_ctx // KV_TILE, 2 * V_DIM, KV_TILE), lambda i, p, j: (i, 0, p, 0)),
        pl.BlockSpec((1, wb, 2 * HEAD_DIM), lambda i, p, j: (i, prev_blk(j), p)),
        pl.BlockSpec((1, 1, 2 * V_DIM, wb), lambda i, p, j: (i, prev_blk(j) // per_tile, p, prev_blk(j) % per_tile)),
        pl.BlockSpec((1, tq, 2 * HEAD_DIM), lambda i, p, j: (i, j, p)),
        pl.BlockSpec((1, tq // KV_TILE, 2 * V_DIM, KV_TILE), lambda i, p, j: (i, j, p, 0)),
        pl.BlockSpec((1, wb, 2 * HEAD_DIM), lambda i, p, j: (i, next_blk(j), p)),
        pl.BlockSpec((1, 1, 2 * V_DIM, wb), lambda i, p, j: (i, next_blk(j) // per_tile, p, next_blk(j) % per_tile)),
        pl.BlockSpec((2, 1, ng), lambda i, p, j: (p, 0, 0)),
    ]
    return pl.pallas_call(
        functools.partial(_attn_window_kernel, tq=tq, n_steps=n_steps),
        grid=(b, KV_HEADS // 2, n_steps), in_specs=in_specs,
        out_specs=pl.BlockSpec((1, pair_rows, tq), lambda i, p, j: (i, p, j)),
        out_shape=jax.ShapeDtypeStruct((b, N_HEADS * HEAD_DIM, t), BF16),
        scratch_shapes=([pltpu.VMEM((1, (n_ctx + 3 * wb) // SUBLANES, SUBLANES, ng), F32)] * (2 * n_sub)
                        + [pltpu.VMEM((n_ctx + 3 * wb, ng), BF16)] * (2 * n_sub)),
        compiler_params=_params(3), name="attn_window",
    )(qT, k_ctx, vT_ctx, k, vT, k, vT, k, vT, sink_cols)


def _post_kernel(*refs, final):
    x_ref, oT_ref, mod_ref, g2_ref, wo_ref, win_ref, wout_ref = refs[:7]
    pos = 7
    if final:
        gf_ref = refs[pos]
        pos += 1
    out_ref = refs[pos]
    mod = mod_ref[0]
    gate, shift2, scale2, gate2 = (mod[:, c * D_MODEL:(c + 1) * D_MODEL] for c in range(2, 6))
    attn = lax.dot_general(oT_ref[0], wo_ref[...], (((0,), (0,)), ((), ())), preferred_element_type=F32)
    x1 = x_ref[0] + gate * attn
    h2 = _norm_mod(x1, g2_ref[...], shift2, scale2).astype(BF16)
    acc = jnp.zeros_like(x1)
    for c in range(D_FF // FF_CHUNK):
        u = jnp.maximum(_dot(h2, win_ref[:, c * FF_CHUNK:(c + 1) * FF_CHUNK]), 0.0)
        acc = acc + _dot((u * u).astype(BF16), wout_ref[c * FF_CHUNK:(c + 1) * FF_CHUNK, :])
    x2 = x1 + gate2 * acc
    out_ref[0] = _rmsnorm(x2, gf_ref[...]) if final else x2


def _post(x, oT, mods, mod_row, g2, wo, win, wout, g_final):
    b, t, _ = x.shape
    tm = min(t, MLP_ROW_TILE)
    final = g_final is not None
    const = lambda i, j: (0, 0)
    in_specs = [pl.BlockSpec((1, tm, D_MODEL), lambda i, j: (i, j, 0)),
                pl.BlockSpec((1, D_MODEL, tm), lambda i, j: (i, 0, j)),
                pl.BlockSpec((1, 1, 6 * D_MODEL), lambda i, j: (mod_row(i), 0, 0)),
                pl.BlockSpec((1, D_MODEL), const),
                pl.BlockSpec((D_MODEL, D_MODEL), const, pipeline_mode=pl.Buffered(1)),
                pl.BlockSpec((D_MODEL, D_FF), const, pipeline_mode=pl.Buffered(1)),
                pl.BlockSpec((D_FF, D_MODEL), const, pipeline_mode=pl.Buffered(1))]
    args = [x, oT, mods, g2.reshape(1, D_MODEL), wo, win, wout]
    if final:
        in_specs.append(pl.BlockSpec((1, D_MODEL), const))
        args.append(g_final.reshape(1, D_MODEL))
    return pl.pallas_call(
        functools.partial(_post_kernel, final=final),
        grid=(b, t // tm), in_specs=in_specs,
        out_specs=pl.BlockSpec((1, tm, D_MODEL), lambda i, j: (i, j, 0)),
        out_shape=jax.ShapeDtypeStruct((b, t, D_MODEL), F32),
        compiler_params=_params(2), name="post_mlp",
    )(*args)


def _sink_cols(sink, width):
    return jnp.repeat((sink.astype(F32) * LOG2_E).reshape(KV_HEADS, 1, GROUP), width, axis=2)


def _mla_weights(w_uq, w_dkv, w_ukv):
    wuq_pad = jnp.pad(w_uq.reshape(Q_LORA, N_HEADS, NOPE_DIM + ROPE_DIM),
                      ((0, 0), (0, 0), (0, MLA_HEAD_PAD - NOPE_DIM - ROPE_DIM)))
    wuq_pad = wuq_pad.reshape(Q_LORA, N_HEADS * MLA_HEAD_PAD).astype(BF16)
    wdkv_pad = jnp.pad(w_dkv, ((0, 0), (0, CKR_PAD - KV_LORA - ROPE_DIM))).astype(BF16)
    w3 = w_ukv.reshape(KV_LORA, N_HEADS, NOPE_DIM + V_DIM)
    wk_nope = jnp.pad(w3[:, :, :NOPE_DIM], ((0, 0), (0, 0), (0, MLA_HEAD_PAD - NOPE_DIM)))
    place = jnp.pad(jnp.eye(ROPE_DIM, dtype=F32), ((0, 0), (NOPE_DIM, MLA_HEAD_PAD - NOPE_DIM - ROPE_DIM)))
    place = jnp.broadcast_to(place[:, None, :], (ROPE_DIM, N_HEADS, MLA_HEAD_PAD))
    wk_exp = jnp.concatenate([wk_nope.reshape(KV_LORA, -1), place.reshape(ROPE_DIM, -1),
                              jnp.zeros((CKR_PAD - KV_LORA - ROPE_DIM, N_HEADS * MLA_HEAD_PAD), F32)], axis=0)
    wv = w3[:, :, NOPE_DIM:].reshape(KV_LORA, N_HEADS * V_DIM)
    return wuq_pad, wdkv_pad, wk_exp.astype(BF16), wv.astype(BF16)


def _ctx_gqa(cache_k, cache_v):
    b, p = cache_k.shape[:2]
    k = cache_k.reshape(b, p, KV_HEADS * HEAD_DIM).astype(BF16)
    v = cache_v.reshape(b, p // KV_TILE, KV_TILE, KV_HEADS * HEAD_DIM)
    return k, jnp.swapaxes(v, 2, 3).astype(BF16)


def kernel(x_prompt, x_sample, c, cache_a_k, cache_a_v, cache_b_ckv, cache_b_krope, cache_c_k, cache_c_v, c_ctx, w_ada, b_ada, norm_g, w_mlp_in, w_mlp_out, a_w_qkv, a_sink, a_w_o, b_w_dq, b_g_q, b_w_uq, b_w_dkv, b_g_kv, b_w_ukv, b_w_o, c_w_qkv, c_g_q, c_g_k, c_w_o, g_final):
    n_batch, seq = x_prompt.shape[:2]
    n_dec, dec_seq = x_sample.shape[:2]
    cond = jnp.concatenate([c_ctx[None, :], c, jnp.zeros((8 - 1 - n_dec, D_MODEL), F32)], axis=0)
    mods_all = _ada_all(cond, w_ada, b_ada)
    tab64 = _rope_tables(dec_seq, HEAD_DIM)
    tab32 = _rope_tables(dec_seq, ROPE_DIM)
    prompt_row = lambda i: 0
    sample_row = lambda i: i + 1

    xp, xs = x_prompt, x_sample
    st = {"a_k": [], "a_v": [], "b_ckv": [], "b_kr": [], "c_k": [], "c_v": []}
    for layer in range(DEPTH):
        kind, j = layer % 3, layer // 3
        mods = mods_all[layer].reshape(8, 1, 6 * D_MODEL)
        g1, g2 = norm_g[layer, 0], norm_g[layer, 1]
        if kind == 0:
            sink = a_sink[j]
            qT, k, vT, ks, vs = _proj_gqa(xp, mods, prompt_row, g1, a_w_qkv[j], None, None, None, states=True)
            st["a_k"].append(ks)
            st["a_v"].append(vs)
            oT_p = _attn_dense(qT, k, vT, _sink_cols(sink, seq), mla=False, tq=seq, pairs=KV_HEADS // 2)
            qT, k, vT = _proj_gqa(xs, mods, sample_row, g1, a_w_qkv[j], tab64, None, None, states=False)
            k_ctx, vT_ctx = _ctx_gqa(cache_a_k[:, j], cache_a_v[:, j])
            oT_s = _attn_window(qT, k_ctx, vT_ctx, k, vT, _sink_cols(sink, WINDOW_BLOCK), tq=512)
            wo = a_w_o[j]
        elif kind == 1:
            wuq_pad, wdkv_pad, wk_exp, wv = _mla_weights(b_w_uq[j], b_w_dkv[j], b_w_ukv[j])
            qT, ck, cs, krs = _proj_mla(xp, mods, prompt_row, g1, b_w_dq[j], b_g_q[j], wuq_pad, wdkv_pad,
                                        b_g_kv[j], None, states=True)
            st["b_ckv"].append(cs)
            st["b_kr"].append(krs)
            oT_p = _attn_dense(qT, *_expand_mla(ck, wk_exp, wv), None, mla=True, tq=seq, pairs=N_HEADS // 2,
                               pair_major=True)
            qT, ck = _proj_mla(xs, mods, sample_row, g1, b_w_dq[j], b_g_q[j], wuq_pad, wdkv_pad,
                               b_g_kv[j], tab32, states=False)
            ck_ctx = jnp.concatenate(
                [cache_b_ckv[:, j], cache_b_krope[:, j],
                 jnp.zeros(cache_b_krope.shape[:1] + cache_b_krope.shape[2:3] + (CKR_PAD - KV_LORA - ROPE_DIM,), F32)],
                axis=-1).astype(BF16)
            ck_all = jnp.concatenate([ck_ctx, ck], axis=1)
            oT_s = _attn_dense(qT, *_expand_mla(ck_all, wk_exp, wv), None, mla=True, tq=2048, pair_major=True)
            wo = b_w_o[j]
        else:
            qT, k, vT, ks, vs = _proj_gqa(xp, mods, prompt_row, g1, c_w_qkv[j], None, c_g_q[j], c_g_k[j],
                                          states=True)
            st["c_k"].append(ks)
            st["c_v"].append(vs)
            oT_p = _attn_dense(qT, k, vT, None, mla=False, tq=seq, pairs=KV_HEADS // 2)
            qT, k, vT = _proj_gqa(xs, mods, sample_row, g1, c_w_qkv[j], tab64, c_g_q[j], c_g_k[j], states=False)
            k_ctx, vT_ctx = _ctx_gqa(cache_c_k[:, j], cache_c_v[:, j])
            oT_s = _attn_dense(qT, jnp.concatenate([k_ctx, k], axis=1), jnp.concatenate([vT_ctx, vT], axis=1),
                               None, mla=False, tq=512)
            wo = c_w_o[j]
        gf = g_final if layer == DEPTH - 1 else None
        wo, win, wout = wo.astype(BF16), w_mlp_in[layer].astype(BF16), w_mlp_out[layer].astype(BF16)
        xp = _post(xp, oT_p, mods, prompt_row, g2, wo, win, wout, gf)
        xs = _post(xs, oT_s, mods, sample_row, g2, wo, win, wout, gf)

    def stack_heads(parts):
        return jnp.stack([p.reshape(n_batch, seq, KV_HEADS, HEAD_DIM) for p in parts], axis=1)

    return (xp, xs, stack_heads(st["a_k"]), stack_heads(st["a_v"]),
            jnp.stack(st["b_ckv"], axis=1), jnp.stack(st["b_kr"], axis=1),
            stack_heads(st["c_k"]), stack_heads(st["c_v"]))
```

```python
import functools

import jax
import jax.numpy as jnp
from jax import lax
from jax.experimental import pallas as pl
from jax.experimental.pallas import tpu as pltpu

D_MODEL = 1024
DEPTH = 4
N_HEADS = 16
KV_HEADS = 4
GROUP = N_HEADS // KV_HEADS
HEAD_DIM = 64
QKV_DIM = (N_HEADS + 2 * KV_HEADS) * HEAD_DIM
Q_LORA = 384
KV_LORA = 256
NOPE_DIM = 64
ROPE_DIM = 32
V_DIM = 64
D_FF = 4 * D_MODEL
GRID_W = 64
WINDOW_BLOCK = 128
ROPE_THETA = 10000.0
EPS = 1e-6
NEG_INF = -1e30
LOG2_E = 1.4426950408889634
ATTN_Q_SCALE = HEAD_DIM ** -0.5 * LOG2_E
MLA_Q_SCALE = (NOPE_DIM + ROPE_DIM) ** -0.5 * LOG2_E

MLA_HEAD_PAD = 128
CKR_PAD = KV_LORA + 128
ROW_TILE = 256
PROJ_STEP_ROWS = 512
MLP_ROW_TILE = 512
KV_TILE = 256
SOFTMAX_CHUNK = 256
SUBLANES = 8
TILE_BLOCKS = 1
TILE_GROUP = 4
SUM_ROWS = 16
FF_CHUNK = 1024
V7X_VMEM_LIMIT_BYTES = 48 * 1024 * 1024

F32 = jnp.float32
BF16 = jnp.bfloat16


def _params(n_axes):
    return pltpu.CompilerParams(dimension_semantics=("arbitrary",) * n_axes,
                                vmem_limit_bytes=V7X_VMEM_LIMIT_BYTES)


def _dot(a, b):
    return jnp.dot(a, b, preferred_element_type=F32)


def _rmsnorm(x, g):
    ms = jnp.mean(x * x, axis=-1, keepdims=True)
    return x * lax.rsqrt(ms + EPS) * g


def _norm_mod(x, g, shift, scale):
    return _rmsnorm(x, g) * (1.0 + scale) + shift


def _ada_kernel(cond_ref, w_ref, b_ref, o_ref):
    cnd = cond_ref[...]
    act = cnd / (1.0 + jnp.exp(-cnd))
    o_ref[0] = _dot(act.astype(BF16), w_ref[0].astype(BF16)) + b_ref[0]


def _ada_all(cond, w_ada, b_ada):
    tn = 1536
    n = 6 * D_MODEL
    return pl.pallas_call(
        _ada_kernel,
        grid=(DEPTH, n // tn),
        in_specs=[pl.BlockSpec((8, D_MODEL), lambda l, j: (0, 0)),
                  pl.BlockSpec((1, D_MODEL, tn), lambda l, j: (l, 0, j)),
                  pl.BlockSpec((1, 1, tn), lambda l, j: (l, 0, j))],
        out_specs=pl.BlockSpec((1, 8, tn), lambda l, j: (l, 0, j)),
        out_shape=jax.ShapeDtypeStruct((DEPTH, 8, n), F32),
        compiler_params=_params(2),
        name="ada_mod",
    )(cond, w_ada, b_ada.reshape(DEPTH, 1, n))


def _rope_t(x3, tab, half):
    a, b = x3[:, 0:half], x3[:, half:2 * half]
    c, d = x3[:, 2 * half:3 * half], x3[:, 3 * half:4 * half]
    cr, sr, cc, sc = tab[0], tab[1], tab[2], tab[3]
    return jnp.concatenate([a * cr - b * sr, a * sr + b * cr, c * cc - d * sc, c * sc + d * cc], axis=1)


def _rope_tables(t_len, dim):
    half = dim // 2
    pos = jnp.arange(t_len)
    rows = (pos // GRID_W).astype(F32)
    cols = (pos % GRID_W).astype(F32)
    freqs = ROPE_THETA ** (-jnp.arange(0, half, 2, dtype=F32) / half)
    ang_r = freqs[:, None] * rows[None, :]
    ang_c = freqs[:, None] * cols[None, :]
    return jnp.stack([jnp.cos(ang_r), jnp.sin(ang_r), jnp.cos(ang_c), jnp.sin(ang_c)])


def _proj_gqa_kernel(*refs, rope, qknorm, states):
    x_ref, mod_ref, g_ref, w_ref = refs[:4]
    pos = 4
    if rope:
        tab_ref = refs[pos]
        pos += 1
    if qknorm:
        gq_ref, gk_ref = refs[pos], refs[pos + 1]
        pos += 2
    qT_ref, k_ref, vT_ref = refs[pos:pos + 3]
    pos += 3
    tm = ROW_TILE
    nq = N_HEADS * HEAD_DIM
    nk = KV_HEADS * HEAD_DIM

    mod = mod_ref[0]
    for sub_tile in range(x_ref.shape[1] // tm):
        rows = slice(sub_tile * tm, (sub_tile + 1) * tm)
        h = _norm_mod(x_ref[0, rows], g_ref[...], mod[:, 0:D_MODEL], mod[:, D_MODEL:2 * D_MODEL])
        y = _dot(h.astype(BF16), w_ref[...])
        q3 = y[:, :nq].T.reshape(N_HEADS, HEAD_DIM, tm)
        k3 = y[:, nq:nq + nk].T.reshape(KV_HEADS, HEAD_DIM, tm)
        v = y[:, nq + nk:]
        if qknorm:
            q3 = q3 * lax.rsqrt(jnp.mean(q3 * q3, axis=1, keepdims=True) + EPS) * gq_ref[...]
            k3 = k3 * lax.rsqrt(jnp.mean(k3 * k3, axis=1, keepdims=True) + EPS) * gk_ref[...]
        if states:
            ks_ref, vs_ref = refs[pos], refs[pos + 1]
            ks_ref[0, rows] = k3.reshape(nk, tm).T if qknorm else y[:, nq:nq + nk]
            vs_ref[0, rows] = v
        if rope:
            tab = tab_ref[:, :, rows]
            q3 = _rope_t(q3, tab, HEAD_DIM // 4)
            k3 = _rope_t(k3, tab, HEAD_DIM // 4)
        qT_ref[0, :, rows] = (q3 * ATTN_Q_SCALE).reshape(nq, tm).astype(BF16)
        k_ref[0, rows] = k3.reshape(nk, tm).T.astype(BF16)
        vT_ref[0, sub_tile] = v.T.astype(BF16)


def _proj_gqa(x, mods, mod_row, g, w, tab, gq, gk, *, states):
    b, t, _ = x.shape
    tm = min(t, PROJ_STEP_ROWS)
    nq = N_HEADS * HEAD_DIM
    nk = KV_HEADS * HEAD_DIM
    rope = tab is not None
    qknorm = gq is not None
    in_specs = [pl.BlockSpec((1, tm, D_MODEL), lambda i, j: (i, j, 0)),
                pl.BlockSpec((1, 1, 6 * D_MODEL), lambda i, j: (mod_row(i), 0, 0)),
                pl.BlockSpec((1, D_MODEL), lambda i, j: (0, 0)),
                pl.BlockSpec((D_MODEL, QKV_DIM), lambda i, j: (0, 0))]
    args = [x, mods, g.reshape(1, D_MODEL), w.astype(BF16)]
    if rope:
        in_specs.append(pl.BlockSpec((4, HEAD_DIM // 4, tm), lambda i, j: (0, 0, j)))
        args.append(tab)
    if qknorm:
        in_specs += [pl.BlockSpec((HEAD_DIM, 1), lambda i, j: (0, 0))] * 2
        args += [gq.reshape(HEAD_DIM, 1), gk.reshape(HEAD_DIM, 1)]
    out_specs = [pl.BlockSpec((1, nq, tm), lambda i, j: (i, 0, j)),
                 pl.BlockSpec((1, tm, nk), lambda i, j: (i, j, 0)),
                 pl.BlockSpec((1, tm // ROW_TILE, nk, ROW_TILE), lambda i, j: (i, j, 0, 0))]
    out_shape = [jax.ShapeDtypeStruct((b, nq, t), BF16),
                 jax.ShapeDtypeStruct((b, t, nk), BF16),
                 jax.ShapeDtypeStruct((b, t // ROW_TILE, nk, ROW_TILE), BF16)]
    if states:
        out_specs += [pl.BlockSpec((1, tm, nk), lambda i, j: (i, j, 0))] * 2
        out_shape += [jax.ShapeDtypeStruct((b, t, nk), F32)] * 2
    return pl.pallas_call(
        functools.partial(_proj_gqa_kernel, rope=rope, qknorm=qknorm, states=states),
        grid=(b, t // tm), in_specs=in_specs, out_specs=out_specs, out_shape=out_shape,
        compiler_params=_params(2), name="proj_gqa",
    )(*args)


def _proj_mla_kernel(*refs, rope, states):
    x_ref, mod_ref, g_ref, wdq_ref, gq_ref, wuq_ref, wdkv_ref, gkv_ref = refs[:8]
    pos = 8
    if rope:
        tab_ref = refs[pos]
        pos += 1
    qT_ref, ck_ref = refs[pos], refs[pos + 1]
    pos += 2
    tm = ROW_TILE
    quarter = ROPE_DIM // 4

    mod = mod_ref[0]
    for sub_tile in range(x_ref.shape[1] // tm):
        rows = slice(sub_tile * tm, (sub_tile + 1) * tm)
        h = _norm_mod(x_ref[0, rows], g_ref[...], mod[:, 0:D_MODEL], mod[:, D_MODEL:2 * D_MODEL]).astype(BF16)
        qn = _rmsnorm(_dot(h, wdq_ref[...]), gq_ref[...])
        q3 = _dot(qn.astype(BF16), wuq_ref[...]).T.reshape(N_HEADS, MLA_HEAD_PAD, tm)
        ckv = _dot(h, wdkv_ref[...])
        cn = _rmsnorm(ckv[:, :KV_LORA], gkv_ref[...])
        kr = ckv[:, KV_LORA:]
        if states:
            cs_ref, krs_ref = refs[pos], refs[pos + 1]
            cs_ref[0, rows] = cn
            krs_ref[0, rows] = kr[:, :ROPE_DIM]
        if rope:
            tab = tab_ref[:, :, rows]
            q_rot = _rope_t(q3[:, NOPE_DIM:NOPE_DIM + ROPE_DIM], tab, quarter)
            q3 = jnp.concatenate([q3[:, :NOPE_DIM], q_rot, q3[:, NOPE_DIM + ROPE_DIM:]], axis=1)
            krT = kr.T
            kr_rot = _rope_t(krT[:ROPE_DIM].reshape(1, ROPE_DIM, tm), tab, quarter).reshape(ROPE_DIM, tm)
            kr = jnp.concatenate([kr_rot, krT[ROPE_DIM:]], axis=0).T
        qT_ref[0, :, rows] = (q3 * MLA_Q_SCALE).reshape(N_HEADS * MLA_HEAD_PAD, tm).astype(BF16)
        ck_ref[0, rows] = jnp.concatenate([cn, kr], axis=1).astype(BF16)


def _proj_mla(x, mods, mod_row, g, wdq, gq, wuq_pad, wdkv_pad, gkv, tab, *, states):
    b, t, _ = x.shape
    tm = min(t, PROJ_STEP_ROWS)
    rope = tab is not None
    nqp = N_HEADS * MLA_HEAD_PAD
    const = lambda i, j: (0, 0)
    in_specs = [pl.BlockSpec((1, tm, D_MODEL), lambda i, j: (i, j, 0)),
                pl.BlockSpec((1, 1, 6 * D_MODEL), lambda i, j: (mod_row(i), 0, 0)),
                pl.BlockSpec((1, D_MODEL), const),
                pl.BlockSpec((D_MODEL, Q_LORA), const),
                pl.BlockSpec((1, Q_LORA), const),
                pl.BlockSpec((Q_LORA, nqp), const),
                pl.BlockSpec((D_MODEL, CKR_PAD), const),
                pl.BlockSpec((1, KV_LORA), const)]
    args = [x, mods, g.reshape(1, D_MODEL), wdq.astype(BF16), gq.reshape(1, Q_LORA), wuq_pad,
            wdkv_pad, gkv.reshape(1, KV_LORA)]
    if rope:
        in_specs.append(pl.BlockSpec((4, ROPE_DIM // 4, tm), lambda i, j: (0, 0, j)))
        args.append(tab)
    out_specs = [pl.BlockSpec((1, nqp, tm), lambda i, j: (i, 0, j)),
                 pl.BlockSpec((1, tm, CKR_PAD), lambda i, j: (i, j, 0))]
    out_shape = [jax.ShapeDtypeStruct((b, nqp, t), BF16),
                 jax.ShapeDtypeStruct((b, t, CKR_PAD), BF16)]
    if states:
        out_specs += [pl.BlockSpec((1, tm, KV_LORA), lambda i, j: (i, j, 0)),
                      pl.BlockSpec((1, tm, ROPE_DIM), lambda i, j: (i, j, 0))]
        out_shape += [jax.ShapeDtypeStruct((b, t, KV_LORA), F32),
                      jax.ShapeDtypeStruct((b, t, ROPE_DIM), F32)]
    return pl.pallas_call(
        functools.partial(_proj_mla_kernel, rope=rope, states=states),
        grid=(b, t // tm), in_specs=in_specs, out_specs=out_specs, out_shape=out_shape,
        compiler_params=_params(2), name="proj_mla",
    )(*args)


def _expand_kernel(ck_ref, wk_ref, wv_ref, k_ref, vT_ref):
    ck = ck_ref[0]
    k = _dot(ck, wk_ref[...]).astype(BF16)
    vT = _dot(ck[:, :KV_LORA], wv_ref[...]).T.astype(BF16)
    for p in range(N_HEADS // 2):
        k_ref[0, p] = k[:, p * 2 * MLA_HEAD_PAD:(p + 1) * 2 * MLA_HEAD_PAD]
        vT_ref[0, p, 0] = vT[p * 2 * V_DIM:(p + 1) * 2 * V_DIM, :]


def _expand_mla(ck, wk_exp, wv):
    b, t, _ = ck.shape
    tm = ROW_TILE
    nkp = N_HEADS * MLA_HEAD_PAD
    nv = N_HEADS * V_DIM
    n_pairs = N_HEADS // 2
    return pl.pallas_call(
        _expand_kernel,
        grid=(b, t // tm),
        in_specs=[pl.BlockSpec((1, tm, CKR_PAD), lambda i, j: (i, j, 0)),
                  pl.BlockSpec((CKR_PAD, nkp), lambda i, j: (0, 0)),
                  pl.BlockSpec((KV_LORA, nv), lambda i, j: (0, 0))],
        out_specs=[pl.BlockSpec((1, n_pairs, tm, 2 * MLA_HEAD_PAD), lambda i, j: (i, 0, j, 0)),
                   pl.BlockSpec((1, n_pairs, 1, 2 * V_DIM, tm), lambda i, j: (i, 0, j, 0, 0))],
        out_shape=[jax.ShapeDtypeStruct((b, n_pairs, t, 2 * MLA_HEAD_PAD), BF16),
                   jax.ShapeDtypeStruct((b, n_pairs, t // tm, 2 * V_DIM, tm), BF16)],
        compiler_params=_params(2), name="expand_mla",
    )(ck, wk_exp, wv)


def _q_operand(q_ref, u, lo, width, group):
    qg = jnp.concatenate(
        [q_ref[0, (u * group + g) * HEAD_DIM:(u * group + g + 1) * HEAD_DIM, lo:lo + width] for g in range(group)],
        axis=1)
    z = jnp.zeros_like(qg)
    return jnp.concatenate([qg, z] if u % 2 == 0 else [z, qg], axis=0)


def _attn_dense_kernel(*refs, mla, units, tq, n_blocks, has_sink):
    q_ref, k_ref, v_ref = refs[:3]
    pos = 3
    sink_ref = None
    if has_sink:
        sink_ref = refs[pos]
        pos += 1
    o_ref = refs[pos]
    pos += 1
    tile_blocks = min(TILE_BLOCKS, n_blocks)
    nbuf = min(TILE_GROUP, pl.cdiv(n_blocks, tile_blocks))
    s_refs = [refs[pos + u * nbuf:pos + (u + 1) * nbuf] for u in range(units)]
    pos += units * nbuf
    p_refs = [refs[pos + u * nbuf:pos + (u + 1) * nbuf] for u in range(units)]
    pos += units * nbuf
    if mla:
        acc_ref, m_ref = refs[pos:]
        group = 1
    else:
        qp_ref, acc_ref, m_ref = refs[pos:]
        group = GROUP
        for u in range(units):
            qp_ref[u] = _q_operand(q_ref, u, 0, tq, group)
    ng = group * tq
    tk = KV_TILE
    cw = SOFTMAX_CHUNK
    sub = SUBLANES
    v_tiles = V_DIM // sub

    for u in range(units):
        m_ref[u] = jnp.broadcast_to(sink_ref[u], (sub, ng)) if has_sink else jnp.full((sub, ng), NEG_INF, F32)
        acc_ref[u, :v_tiles] = jnp.zeros((v_tiles, sub, ng), F32)
        acc_ref[u, v_tiles:] = jnp.full((SUM_ROWS // sub, sub, ng), 1.0 if has_sink else 0.0, F32)

    def scores(b0, nb, par):
        rows = nb * tk
        off = b0 * tk if isinstance(b0, int) else pl.multiple_of(b0 * tk, tk)
        for u in range(units):
            if mla:
                kp = k_ref[0, u // 2, pl.ds(off, rows), (u % 2) * MLA_HEAD_PAD:(u % 2 + 1) * MLA_HEAD_PAD]
                qop = q_ref[0, u * MLA_HEAD_PAD:(u + 1) * MLA_HEAD_PAD, :]
            else:
                kp = k_ref[0, u // 2, pl.ds(off, rows), :]
                qop = qp_ref[u]
            s_refs[u][par][:rows // sub] = _dot(kp, qop).reshape(rows // sub, sub, ng)

    def accumulate(b0, nb, par):
        rows = nb * tk
        for u in range(units):
            s_ref, p_ref = s_refs[u][par], p_refs[u][par]
            m_blk = jnp.max(s_ref[:rows // sub], axis=0)
            for shift in (4, 2, 1):
                m_blk = jnp.maximum(m_blk, pltpu.roll(m_blk, shift, 0))
            m_prev = m_ref[u]
            m_new = jnp.maximum(m_prev, m_blk)
            m_ref[u] = m_new
            alpha = jnp.exp2(m_prev - m_new)
            for c in range(ng // cw):
                cols = slice(c * cw, (c + 1) * cw)
                p = jnp.exp2(s_ref[:rows // sub, :, cols] - m_ref[u, :, cols])
                p_ref[:rows, cols] = p.reshape(rows, cw).astype(BF16)
            v_rows = [v_ref[0, u // 2, b0 + i, (u % 2) * V_DIM:(u % 2 + 1) * V_DIM, :] for i in range(nb)]
            v_ones = jnp.concatenate([jnp.concatenate(v_rows, axis=1), jnp.ones((SUM_ROWS, rows), BF16)], axis=0)
            pv = _dot(v_ones, p_ref[:rows]).reshape(v_tiles + SUM_ROWS // sub, sub, ng)
            acc_ref[u] = alpha * acc_ref[u] + pv

    def tile_group(b0, sizes):
        starts = [b0 + sum(sizes[:r]) for r in range(len(sizes))]
        for r, nb in enumerate(sizes):
            scores(starts[r], nb, r)
        for r, nb in enumerate(sizes):
            accumulate(starts[r], nb, r)

    tiles = [n_blocks % tile_blocks] * bool(n_blocks % tile_blocks) + [tile_blocks] * (n_blocks // tile_blocks)
    head = len(tiles) % nbuf
    if head == 0 and tiles[0] != tile_blocks:
        head = nbuf
    if head:
        tile_group(0, tiles[:head])
    first = sum(tiles[:head])
    n_loop = (len(tiles) - head) // nbuf
    if n_loop == 1:
        tile_group(first, tiles[head:])
    elif n_loop > 1:
        def body(i, carry):
            tile_group(first + nbuf * tile_blocks * i, [tile_blocks] * nbuf)
            return carry
        lax.fori_loop(0, n_loop, body, 0)

    for u in range(units):
        o = (acc_ref[u, :v_tiles] / acc_ref[u, v_tiles]).reshape(V_DIM, ng)
        if mla:
            o_ref[0, u * V_DIM:(u + 1) * V_DIM, :] = o.astype(BF16)
        else:
            for g in range(group):
                r0 = (u * group + g) * HEAD_DIM
                o_ref[0, r0:r0 + HEAD_DIM, :] = o[:, g * tq:(g + 1) * tq].astype(BF16)


def _attn_dense(qT, k, vT, sink_cols, *, mla, tq, pairs=1, pair_major=False):
    b, _, t = qT.shape
    tk = KV_TILE
    n = k.shape[2] if pair_major else k.shape[1]
    n_pairs = N_HEADS // 2 if mla else KV_HEADS // 2
    units = 2 * pairs
    q_rows = pairs * (2 * MLA_HEAD_PAD if mla else 2 * GROUP * HEAD_DIM)
    o_rows = pairs * (2 * V_DIM if mla else 2 * GROUP * HEAD_DIM)
    pair_cols = 2 * MLA_HEAD_PAD if mla else 2 * HEAD_DIM
    ng = tq if mla else GROUP * tq
    tile_blocks = min(TILE_BLOCKS, n // tk)
    nbuf = min(TILE_GROUP, pl.cdiv(n // tk, tile_blocks))
    tile_rows = tile_blocks * tk
    k_pm = k if pair_major else jnp.swapaxes(k.reshape(b, n, n_pairs, pair_cols), 1, 2)
    v_pm = vT if pair_major else jnp.swapaxes(vT.reshape(b, n // tk, n_pairs, 2 * V_DIM, tk), 1, 2)
    in_specs = [pl.BlockSpec((1, q_rows, tq), lambda i, p, j: (i, p, j)),
                pl.BlockSpec((1, pairs, n, pair_cols), lambda i, p, j: (i, p, 0, 0)),
                pl.BlockSpec((1, pairs, n // tk, 2 * V_DIM, tk), lambda i, p, j: (i, p, 0, 0, 0))]
    args = [qT, k_pm, v_pm]
    if sink_cols is not None:
        in_specs.append(pl.BlockSpec((units, 1, ng), lambda i, p, j: (p, 0, 0)))
        args.append(sink_cols)
    sub = SUBLANES
    scratch = [pltpu.VMEM((tile_rows // sub, sub, ng), F32)] * (units * nbuf)
    scratch += [pltpu.VMEM((tile_rows, ng), BF16)] * (units * nbuf)
    if not mla:
        scratch.append(pltpu.VMEM((units, 2 * HEAD_DIM, ng), BF16))
    scratch += [pltpu.VMEM((units, (V_DIM + SUM_ROWS) // sub, sub, ng), F32),
                pltpu.VMEM((units, sub, ng), F32)]
    return pl.pallas_call(
        functools.partial(_attn_dense_kernel, mla=mla, units=units, tq=tq, n_blocks=n // tk,
                          has_sink=sink_cols is not None),
        grid=(b, n_pairs // pairs, t // tq), in_specs=in_specs,
        out_specs=pl.BlockSpec((1, o_rows, tq), lambda i, p, j: (i, p, j)),
        out_shape=jax.ShapeDtypeStruct((b, N_HEADS * V_DIM, t), BF16),
        scratch_shapes=scratch, compiler_params=_params(3),
        name="attn_mla" if mla else "attn_gqa",
    )(*args)


def _attn_window_kernel(q_ref, kc_ref, vc_ref, kp_ref, vp_ref, kcur_ref, vcur_ref, kn_ref, vn_ref,
                        sink_ref, o_ref, *scratch, tq, n_steps):
    wb = WINDOW_BLOCK
    step = pl.program_id(2)
    n_sub = tq // wb
    s_refs, p_refs = scratch[:2 * n_sub], scratch[2 * n_sub:]
    ng = GROUP * wb
    k_off = lax.broadcasted_iota(jnp.int32, (wb, ng), 0)
    q_off = lax.broadcasted_iota(jnp.int32, (wb, ng), 1) % wb
    keep_prev = k_off >= q_off
    keep_next = k_off <= q_off
    first = step == 0
    last = step == n_steps - 1
    n_ctx = kc_ref.shape[1]
    ones_rows = jnp.ones((SUM_ROWS, n_ctx + 3 * wb), BF16)

    def kv_block(i):
        if i < 0:
            return kp_ref[0], vp_ref[0, 0]
        if i >= n_sub:
            return kn_ref[0], vn_ref[0, 0]
        c, r = divmod(i * wb, KV_TILE)
        return kcur_ref[0, i * wb:(i + 1) * wb, :], vcur_ref[0, c, :, r:r + wb]

    sub = SUBLANES
    n_keys = n_ctx + 3 * wb
    v_tiles = V_DIM // sub

    def tiles(x):
        return x.reshape(x.shape[0] // sub, sub, x.shape[1])

    dz = jnp.minimum(step, 0)

    for i in range(n_sub):
        kcat = jnp.concatenate([kc_ref[0], kv_block(i - 1)[0], kv_block(i)[0], kv_block(i + 1)[0]], axis=0)
        keep0 = jnp.logical_and(keep_prev, jnp.logical_not(first)) if i == 0 else keep_prev
        keep2 = jnp.logical_and(keep_next, jnp.logical_not(last)) if i == n_sub - 1 else keep_next
        for u in range(2):
            s = _dot(kcat, _q_operand(q_ref, u, i * wb, wb, GROUP))
            s_ref = s_refs[2 * i + u]
            t0, t1, t2 = n_ctx // sub, (n_ctx + wb) // sub, (n_ctx + 2 * wb) // sub
            s_ref[dz, :t0] = tiles(s[:n_ctx])
            s_ref[dz, t0:t1] = tiles(jnp.where(keep0, s[n_ctx:n_ctx + wb], NEG_INF))
            s_ref[dz, t1:t2] = tiles(s[n_ctx + wb:n_ctx + 2 * wb])
            s_ref[dz, t2:] = tiles(jnp.where(keep2, s[n_ctx + 2 * wb:], NEG_INF))

    for i in range(n_sub):
        vcat = jnp.concatenate([vc_ref[0, 0], kv_block(i - 1)[1], kv_block(i)[1], kv_block(i + 1)[1]], axis=1)
        for u in range(2):
            s_ref, p_ref = s_refs[2 * i + u], p_refs[2 * i + u]
            sink = jnp.broadcast_to(sink_ref[u], (sub, ng))
            m = jnp.max(s_ref[dz], axis=0)
            for shift in (4, 2, 1):
                m = jnp.maximum(m, pltpu.roll(m, shift, 0))
            m = jnp.maximum(m, sink)
            p_ref[...] = jnp.exp2(s_ref[dz] - m).reshape(n_keys, ng).astype(BF16)
            v_ones = jnp.concatenate([vcat[u * V_DIM:(u + 1) * V_DIM, :], ones_rows], axis=0)
            r = tiles(_dot(v_ones, p_ref[...]))
            o = (r[:v_tiles] / (r[v_tiles] + jnp.exp2(sink - m))).reshape(V_DIM, ng)
            for g in range(GROUP):
                r0 = (u * GROUP + g) * HEAD_DIM
                o_ref[0, r0:r0 + HEAD_DIM, i * wb:(i + 1) * wb] = o[:, g * wb:(g + 1) * wb].astype(BF16)


def _attn_window(qT, k_ctx, vT_ctx, k, vT, sink_cols, *, tq):
    b, _, t = qT.shape
    wb = WINDOW_BLOCK
    n_ctx = k_ctx.shape[1]
    n_steps = t // tq
    n_sub = tq // wb
    n_wb = t // wb
    per_tile = KV_TILE // wb
    pair_rows = 2 * GROUP * HEAD_DIM
    ng = GROUP * wb

    def prev_blk(j):
        return jnp.maximum(j * n_sub - 1, 0)

    def next_blk(j):
        return jnp.minimum((j + 1) * n_sub, n_wb - 1)

    in_specs = [
        pl.BlockSpec((1, pair_rows, tq), lambda i, p, j: (i, p, j)),
        pl.BlockSpec((1, n_ctx, 2 * HEAD_DIM), lambda i, p, j: (i, 0, p)),
        pl.BlockSpec((1, n_ctx // KV_TILE, 2 * V_DIM, KV_TILE), lambda i, p, j: (i, 0, p, 0)),
        pl.BlockSpec((1, wb, 2 * HEAD_DIM), lambda i, p, j: (i, prev_blk(j), p)),
        pl.BlockSpec((1, 1, 2 * V_DIM, wb), lambda i, p, j: (i, prev_blk(j) // per_tile, p, prev_blk(j) % per_tile)),
        pl.BlockSpec((1, tq, 2 * HEAD_DIM), lambda i, p, j: (i, j, p)),
        pl.BlockSpec((1, tq // KV_TILE, 2 * V_DIM, KV_TILE), lambda i, p, j: (i, j, p, 0)),
        pl.BlockSpec((1, wb, 2 * HEAD_DIM), lambda i, p, j: (i, next_blk(j), p)),
        pl.BlockSpec((1, 1, 2 * V_DIM, wb), lambda i, p, j: (i, next_blk(j) // per_tile, p, next_blk(j) % per_tile)),
        pl.BlockSpec((2, 1, ng), lambda i, p, j: (p, 0, 0)),
    ]
    return pl.pallas_call(
        functools.partial(_attn_window_kernel, tq=tq, n_steps=n_steps),
        grid=(b, KV_HEADS // 2, n_steps), in_specs=in_specs,
        out_specs=pl.BlockSpec((1, pair_rows, tq), lambda i, p, j: (i, p, j)),
        out_shape=jax.ShapeDtypeStruct((b, N_HEADS * HEAD_DIM, t), BF16),
        scratch_shapes=([pltpu.VMEM((1, (n_ctx + 3 * wb) // SUBLANES, SUBLANES, ng), F32)] * (2 * n_sub)
                        + [pltpu.VMEM((n_ctx + 3 * wb, ng), BF16)] * (2 * n_sub)),
        compiler_params=_params(3), name="attn_window",
    )(qT, k_ctx, vT_ctx, k, vT, k, vT, k, vT, sink_cols)


def _post_kernel(*refs, final):
    x_ref, oT_ref, mod_ref, g2_ref, wo_ref, win_ref, wout_ref = refs[:7]
    pos = 7
    if final:
        gf_ref = refs[pos]
        pos += 1
    out_ref = refs[pos]
    mod = mod_ref[0]
    gate, shift2, scale2, gate2 = (mod[:, c * D_MODEL:(c + 1) * D_MODEL] for c in range(2, 6))
    attn = lax.dot_general(oT_ref[0], wo_ref[...], (((0,), (0,)), ((), ())), preferred_element_type=F32)
    x1 = x_ref[0] + gate * attn
    h2 = _norm_mod(x1, g2_ref[...], shift2, scale2).astype(BF16)
    acc = jnp.zeros_like(x1)
    for c in range(D_FF // FF_CHUNK):
        u = jnp.maximum(_dot(h2, win_ref[:, c * FF_CHUNK:(c + 1) * FF_CHUNK]), 0.0)
        acc = acc + _dot((u * u).astype(BF16), wout_ref[c * FF_CHUNK:(c + 1) * FF_CHUNK, :])
    x2 = x1 + gate2 * acc
    out_ref[0] = _rmsnorm(x2, gf_ref[...]) if final else x2


def _post(x, oT, mods, mod_row, g2, wo, win, wout, g_final):
    b, t, _ = x.shape
    tm = min(t, MLP_ROW_TILE)
    final = g_final is not None
    const = lambda i, j: (0, 0)
    in_specs = [pl.BlockSpec((1, tm, D_MODEL), lambda i, j: (i, j, 0)),
                pl.BlockSpec((1, D_MODEL, tm), lambda i, j: (i, 0, j)),
                pl.BlockSpec((1, 1, 6 * D_MODEL), lambda i, j: (mod_row(i), 0, 0)),
                pl.BlockSpec((1, D_MODEL), const),
                pl.BlockSpec((D_MODEL, D_MODEL), const, pipeline_mode=pl.Buffered(1)),
                pl.BlockSpec((D_MODEL, D_FF), const, pipeline_mode=pl.Buffered(1)),
                pl.BlockSpec((D_FF, D_MODEL), const, pipeline_mode=pl.Buffered(1))]
    args = [x, oT, mods, g2.reshape(1, D_MODEL), wo, win, wout]
    if final:
        in_specs.append(pl.BlockSpec((1, D_MODEL), const))
        args.append(g_final.reshape(1, D_MODEL))
    return pl.pallas_call(
        functools.partial(_post_kernel, final=final),
        grid=(b, t // tm), in_specs=in_specs,
        out_specs=pl.BlockSpec((1, tm, D_MODEL), lambda i, j: (i, j, 0)),
        out_shape=jax.ShapeDtypeStruct((b, t, D_MODEL), F32),
        compiler_params=_params(2), name="post_mlp",
    )(*args)


def _sink_cols(sink, width):
    return jnp.repeat((sink.astype(F32) * LOG2_E).reshape(KV_HEADS, 1, GROUP), width, axis=2)


def _mla_weights(w_uq, w_dkv, w_ukv):
    wuq_pad = jnp.pad(w_uq.reshape(Q_LORA, N_HEADS, NOPE_DIM + ROPE_DIM),
                      ((0, 0), (0, 0), (0, MLA_HEAD_PAD - NOPE_DIM - ROPE_DIM)))
    wuq_pad = wuq_pad.reshape(Q_LORA, N_HEADS * MLA_HEAD_PAD).astype(BF16)
    wdkv_pad = jnp.pad(w_dkv, ((0, 0), (0, CKR_PAD - KV_LORA - ROPE_DIM))).astype(BF16)
    w3 = w_ukv.reshape(KV_LORA, N_HEADS, NOPE_DIM + V_DIM)
    wk_nope = jnp.pad(w3[:, :, :NOPE_DIM], ((0, 0), (0, 0), (0, MLA_HEAD_PAD - NOPE_DIM)))
    place = jnp.pad(jnp.eye(ROPE_DIM, dtype=F32), ((0, 0), (NOPE_DIM, MLA_HEAD_PAD - NOPE_DIM - ROPE_DIM)))
    place = jnp.broadcast_to(place[:, None, :], (ROPE_DIM, N_HEADS, MLA_HEAD_PAD))
    wk_exp = jnp.concatenate([wk_nope.reshape(KV_LORA, -1), place.reshape(ROPE_DIM, -1),
                              jnp.zeros((CKR_PAD - KV_LORA - ROPE_DIM, N_HEADS * MLA_HEAD_PAD), F32)], axis=0)
    wv = w3[:, :, NOPE_DIM:].reshape(KV_LORA, N_HEADS * V_DIM)
    return wuq_pad, wdkv_pad, wk_exp.astype(BF16), wv.astype(BF16)


def _ctx_gqa(cache_k, cache_v):
    b, p = cache_k.shape[:2]
    k = cache_k.reshape(b, p, KV_HEADS * HEAD_DIM).astype(BF16)
    v = cache_v.reshape(b, p // KV_TILE, KV_TILE, KV_HEADS * HEAD_DIM)
    return k, jnp.swapaxes(v, 2, 3).astype(BF16)


def kernel(x_prompt, x_sample, c, cache_a_k, cache_a_v, cache_b_ckv, cache_b_krope, cache_c_k, cache_c_v, c_ctx, w_ada, b_ada, norm_g, w_mlp_in, w_mlp_out, a_w_qkv, a_sink, a_w_o, b_w_dq, b_g_q, b_w_uq, b_w_dkv, b_g_kv, b_w_ukv, b_w_o, c_w_qkv, c_g_q, c_g_k, c_w_o, g_final):
    n_batch, seq = x_prompt.shape[:2]
    n_dec, dec_seq = x_sample.shape[:2]
    cond = jnp.concatenate([c_ctx[None, :], c, jnp.zeros((8 - 1 - n_dec, D_MODEL), F32)], axis=0)
    mods_all = _ada_all(cond, w_ada, b_ada)
    tab64 = _rope_tables(dec_seq, HEAD_DIM)
    tab32 = _rope_tables(dec_seq, ROPE_DIM)
    prompt_row = lambda i: 0
    sample_row = lambda i: i + 1

    xp, xs = x_prompt, x_sample
    st = {"a_k": [], "a_v": [], "b_ckv": [], "b_kr": [], "c_k": [], "c_v": []}
    for layer in range(DEPTH):
        kind, j = layer % 3, layer // 3
        mods = mods_all[layer].reshape(8, 1, 6 * D_MODEL)
        g1, g2 = norm_g[layer, 0], norm_g[layer, 1]
        if kind == 0:
            sink = a_sink[j]
            qT, k, vT, ks, vs = _proj_gqa(xp, mods, prompt_row, g1, a_w_qkv[j], None, None, None, states=True)
            st["a_k"].append(ks)
            st["a_v"].append(vs)
            oT_p = _attn_dense(qT, k, vT, _sink_cols(sink, seq), mla=False, tq=seq, pairs=KV_HEADS // 2)
            qT, k, vT = _proj_gqa(xs, mods, sample_row, g1, a_w_qkv[j], tab64, None, None, states=False)
            k_ctx, vT_ctx = _ctx_gqa(cache_a_k[:, j], cache_a_v[:, j])
            oT_s = _attn_window(qT, k_ctx, vT_ctx, k, vT, _sink_cols(sink, WINDOW_BLOCK), tq=1024)
            wo = a_w_o[j]
        elif kind == 1:
            wuq_pad, wdkv_pad, wk_exp, wv = _mla_weights(b_w_uq[j], b_w_dkv[j], b_w_ukv[j])
            qT, ck, cs, krs = _proj_mla(xp, mods, prompt_row, g1, b_w_dq[j], b_g_q[j], wuq_pad, wdkv_pad,
                                        b_g_kv[j], None, states=True)
            st["b_ckv"].append(cs)
            st["b_kr"].append(krs)
            oT_p = _attn_dense(qT, *_expand_mla(ck, wk_exp, wv), None, mla=True, tq=seq, pairs=N_HEADS // 2,
                               pair_major=True)
            qT, ck = _proj_mla(xs, mods, sample_row, g1, b_w_dq[j], b_g_q[j], wuq_pad, wdkv_pad,
                               b_g_kv[j], tab32, states=False)
            ck_ctx = jnp.concatenate(
                [cache_b_ckv[:, j], cache_b_krope[:, j],
                 jnp.zeros(cache_b_krope.shape[:1] + cache_b_krope.shape[2:3] + (CKR_PAD - KV_LORA - ROPE_DIM,), F32)],
                axis=-1).astype(BF16)
            ck_all = jnp.concatenate([ck_ctx, ck], axis=1)
            oT_s = _attn_dense(qT, *_expand_mla(ck_all, wk_exp, wv), None, mla=True, tq=2048, pair_major=True)
            wo = b_w_o[j]
        else:
            qT, k, vT, ks, vs = _proj_gqa(xp, mods, prompt_row, g1, c_w_qkv[j], None, c_g_q[j], c_g_k[j],
                                          states=True)
            st["c_k"].append(ks)
            st["c_v"].append(vs)
            oT_p = _attn_dense(qT, k, vT, None, mla=False, tq=seq, pairs=KV_HEADS // 2)
            qT, k, vT = _proj_gqa(xs, mods, sample_row, g1, c_w_qkv[j], tab64, c_g_q[j], c_g_k[j], states=False)
            k_ctx, vT_ctx = _ctx_gqa(cache_c_k[:, j], cache_c_v[:, j])
            oT_s = _attn_dense(qT, jnp.concatenate([k_ctx, k], axis=1), jnp.concatenate([vT_ctx, vT], axis=1),
                               None, mla=False, tq=512)
            wo = c_w_o[j]
        gf = g_final if layer == DEPTH - 1 else None
        wo, win, wout = wo.astype(BF16), w_mlp_in[layer].astype(BF16), w_mlp_out[layer].astype(BF16)
        xp = _post(xp, oT_p, mods, prompt_row, g2, wo, win, wout, gf)
        xs = _post(xs, oT_s, mods, sample_row, g2, wo, win, wout, gf)

    def stack_heads(parts):
        return jnp.stack([p.reshape(n_batch, seq, KV_HEADS, HEAD_DIM) for p in parts], axis=1)

    return (xp, xs, stack_heads(st["a_k"]), stack_heads(st["a_v"]),
            jnp.stack(st["b_ckv"], axis=1), jnp.stack(st["b_kr"], axis=1),
            stack_heads(st["c_k"]), stack_heads(st["c_v"]))
```

```python
import functools

import jax
import jax.numpy as jnp
from jax import lax
from jax.experimental import pallas as pl
from jax.experimental.pallas import tpu as pltpu

D_MODEL = 1024
DEPTH = 4
N_HEADS = 16
KV_HEADS = 4
GROUP = N_HEADS // KV_HEADS
HEAD_DIM = 64
QKV_DIM = (N_HEADS + 2 * KV_HEADS) * HEAD_DIM
Q_LORA = 384
KV_LORA = 256
NOPE_DIM = 64
ROPE_DIM = 32
V_DIM = 64
D_FF = 4 * D_MODEL
GRID_W = 64
WINDOW_BLOCK = 128
ROPE_THETA = 10000.0
EPS = 1e-6
NEG_INF = -1e30
LOG2_E = 1.4426950408889634
ATTN_Q_SCALE = HEAD_DIM ** -0.5 * LOG2_E
MLA_Q_SCALE = (NOPE_DIM + ROPE_DIM) ** -0.5 * LOG2_E

MLA_HEAD_PAD = 128
CKR_PAD = KV_LORA + 128
ROW_TILE = 256
PROJ_STEP_ROWS = 512
MLP_ROW_TILE = 512
KV_TILE = 256
SOFTMAX_CHUNK = 256
SUBLANES = 8
TILE_BLOCKS = 1
TILE_GROUP = 4
SUM_ROWS = 16
FF_CHUNK = 1024
V7X_VMEM_LIMIT_BYTES = 48 * 1024 * 1024

F32 = jnp.float32
BF16 = jnp.bfloat16


def _params(n_axes):
    return pltpu.CompilerParams(dimension_semantics=("arbitrary",) * n_axes,
                                vmem_limit_bytes=V7X_VMEM_LIMIT_BYTES)


def _dot(a, b):
    return jnp.dot(a, b, preferred_element_type=F32)


def _rmsnorm(x, g):
    ms = jnp.mean(x * x, axis=-1, keepdims=True)
    return x * lax.rsqrt(ms + EPS) * g


def _norm_mod(x, g, shift, scale):
    return _rmsnorm(x, g) * (1.0 + scale) + shift


def _ada_kernel(cond_ref, w_ref, b_ref, o_ref):
    cnd = cond_ref[...]
    act = cnd / (1.0 + jnp.exp(-cnd))
    o_ref[0] = _dot(act.astype(BF16), w_ref[0].astype(BF16)) + b_ref[0]


def _ada_all(cond, w_ada, b_ada):
    tn = 1536
    n = 6 * D_MODEL
    return pl.pallas_call(
        _ada_kernel,
        grid=(DEPTH, n // tn),
        in_specs=[pl.BlockSpec((8, D_MODEL), lambda l, j: (0, 0)),
                  pl.BlockSpec((1, D_MODEL, tn), lambda l, j: (l, 0, j)),
                  pl.BlockSpec((1, 1, tn), lambda l, j: (l, 0, j))],
        out_specs=pl.BlockSpec((1, 8, tn), lambda l, j: (l, 0, j)),
        out_shape=jax.ShapeDtypeStruct((DEPTH, 8, n), F32),
        compiler_params=_params(2),
        name="ada_mod",
    )(cond, w_ada, b_ada.reshape(DEPTH, 1, n))


def _rope_t(x3, tab, half):
    a, b = x3[:, 0:half], x3[:, half:2 * half]
    c, d = x3[:, 2 * half:3 * half], x3[:, 3 * half:4 * half]
    cr, sr, cc, sc = tab[0], tab[1], tab[2], tab[3]
    return jnp.concatenate([a * cr - b * sr, a * sr + b * cr, c * cc - d * sc, c * sc + d * cc], axis=1)


def _rope_tables(t_len, dim):
    half = dim // 2
    pos = jnp.arange(t_len)
    rows = (pos // GRID_W).astype(F32)
    cols = (pos % GRID_W).astype(F32)
    freqs = ROPE_THETA ** (-jnp.arange(0, half, 2, dtype=F32) / half)
    ang_r = freqs[:, None] * rows[None, :]
    ang_c = freqs[:, None] * cols[None, :]
    return jnp.stack([jnp.cos(ang_r), jnp.sin(ang_r), jnp.cos(ang_c), jnp.sin(ang_c)])


def _proj_gqa_kernel(*refs, rope, qknorm, states):
    x_ref, mod_ref, g_ref, w_ref = refs[:4]
    pos = 4
    if rope:
        tab_ref = refs[pos]
        pos += 1
    if qknorm:
        gq_ref, gk_ref = refs[pos], refs[pos + 1]
        pos += 2
    qT_ref, k_ref, vT_ref = refs[pos:pos + 3]
    pos += 3
    tm = ROW_TILE
    nq = N_HEADS * HEAD_DIM
    nk = KV_HEADS * HEAD_DIM

    mod = mod_ref[0]
    for sub_tile in range(x_ref.shape[1] // tm):
        rows = slice(sub_tile * tm, (sub_tile + 1) * tm)
        h = _norm_mod(x_ref[0, rows], g_ref[...], mod[:, 0:D_MODEL], mod[:, D_MODEL:2 * D_MODEL])
        y = _dot(h.astype(BF16), w_ref[...])
        q3 = y[:, :nq].T.reshape(N_HEADS, HEAD_DIM, tm)
        k3 = y[:, nq:nq + nk].T.reshape(KV_HEADS, HEAD_DIM, tm)
        v = y[:, nq + nk:]
        if qknorm:
            q3 = q3 * lax.rsqrt(jnp.mean(q3 * q3, axis=1, keepdims=True) + EPS) * gq_ref[...]
            k3 = k3 * lax.rsqrt(jnp.mean(k3 * k3, axis=1, keepdims=True) + EPS) * gk_ref[...]
        if states:
            ks_ref, vs_ref = refs[pos], refs[pos + 1]
            ks_ref[0, rows] = k3.reshape(nk, tm).T if qknorm else y[:, nq:nq + nk]
            vs_ref[0, rows] = v
        if rope:
            tab = tab_ref[:, :, rows]
            q3 = _rope_t(q3, tab, HEAD_DIM // 4)
            k3 = _rope_t(k3, tab, HEAD_DIM // 4)
        qT_ref[0, :, rows] = (q3 * ATTN_Q_SCALE).reshape(nq, tm).astype(BF16)
        k_ref[0, rows] = k3.reshape(nk, tm).T.astype(BF16)
        vT_ref[0, sub_tile] = v.T.astype(BF16)


def _proj_gqa(x, mods, mod_row, g, w, tab, gq, gk, *, states):
    b, t, _ = x.shape
    tm = min(t, PROJ_STEP_ROWS)
    nq = N_HEADS * HEAD_DIM
    nk = KV_HEADS * HEAD_DIM
    rope = tab is not None
    qknorm = gq is not None
    in_specs = [pl.BlockSpec((1, tm, D_MODEL), lambda i, j: (i, j, 0)),
                pl.BlockSpec((1, 1, 6 * D_MODEL), lambda i, j: (mod_row(i), 0, 0)),
                pl.BlockSpec((1, D_MODEL), lambda i, j: (0, 0)),
                pl.BlockSpec((D_MODEL, QKV_DIM), lambda i, j: (0, 0))]
    args = [x, mods, g.reshape(1, D_MODEL), w.astype(BF16)]
    if rope:
        in_specs.append(pl.BlockSpec((4, HEAD_DIM // 4, tm), lambda i, j: (0, 0, j)))
        args.append(tab)
    if qknorm:
        in_specs += [pl.BlockSpec((HEAD_DIM, 1), lambda i, j: (0, 0))] * 2
        args += [gq.reshape(HEAD_DIM, 1), gk.reshape(HEAD_DIM, 1)]
    out_specs = [pl.BlockSpec((1, nq, tm), lambda i, j: (i, 0, j)),
                 pl.BlockSpec((1, tm, nk), lambda i, j: (i, j, 0)),
                 pl.BlockSpec((1, tm // ROW_TILE, nk, ROW_TILE), lambda i, j: (i, j, 0, 0))]
    out_shape = [jax.ShapeDtypeStruct((b, nq, t), BF16),
                 jax.ShapeDtypeStruct((b, t, nk), BF16),
                 jax.ShapeDtypeStruct((b, t // ROW_TILE, nk, ROW_TILE), BF16)]
    if states:
        out_specs += [pl.BlockSpec((1, tm, nk), lambda i, j: (i, j, 0))] * 2
        out_shape += [jax.ShapeDtypeStruct((b, t, nk), F32)] * 2
    return pl.pallas_call(
        functools.partial(_proj_gqa_kernel, rope=rope, qknorm=qknorm, states=states),
        grid=(b, t // tm), in_specs=in_specs, out_specs=out_specs, out_shape=out_shape,
        compiler_params=_params(2), name="proj_gqa",
    )(*args)


def _proj_mla_kernel(*refs, rope, states):
    x_ref, mod_ref, g_ref, wdq_ref, gq_ref, wuq_ref, wdkv_ref, gkv_ref = refs[:8]
    pos = 8
    if rope:
        tab_ref = refs[pos]
        pos += 1
    qT_ref, ck_ref = refs[pos], refs[pos + 1]
    pos += 2
    tm = ROW_TILE
    quarter = ROPE_DIM // 4

    mod = mod_ref[0]
    for sub_tile in range(x_ref.shape[1] // tm):
        rows = slice(sub_tile * tm, (sub_tile + 1) * tm)
        h = _norm_mod(x_ref[0, rows], g_ref[...], mod[:, 0:D_MODEL], mod[:, D_MODEL:2 * D_MODEL]).astype(BF16)
        qn = _rmsnorm(_dot(h, wdq_ref[...]), gq_ref[...])
        q3 = _dot(qn.astype(BF16), wuq_ref[...]).T.reshape(N_HEADS, MLA_HEAD_PAD, tm)
        ckv = _dot(h, wdkv_ref[...])
        cn = _rmsnorm(ckv[:, :KV_LORA], gkv_ref[...])
        kr = ckv[:, KV_LORA:]
        if states:
            cs_ref, krs_ref = refs[pos], refs[pos + 1]
            cs_ref[0, rows] = cn
            krs_ref[0, rows] = kr[:, :ROPE_DIM]
        if rope:
            tab = tab_ref[:, :, rows]
            q_rot = _rope_t(q3[:, NOPE_DIM:NOPE_DIM + ROPE_DIM], tab, quarter)
            q3 = jnp.concatenate([q3[:, :NOPE_DIM], q_rot, q3[:, NOPE_DIM + ROPE_DIM:]], axis=1)
            krT = kr.T
            kr_rot = _rope_t(krT[:ROPE_DIM].reshape(1, ROPE_DIM, tm), tab, quarter).reshape(ROPE_DIM, tm)
            kr = jnp.concatenate([kr_rot, krT[ROPE_DIM:]], axis=0).T
        qT_ref[0, :, rows] = (q3 * MLA_Q_SCALE).reshape(N_HEADS * MLA_HEAD_PAD, tm).astype(BF16)
        ck_ref[0, rows] = jnp.concatenate([cn, kr], axis=1).astype(BF16)


def _proj_mla(x, mods, mod_row, g, wdq, gq, wuq_pad, wdkv_pad, gkv, tab, *, states):
    b, t, _ = x.shape
    tm = min(t, PROJ_STEP_ROWS)
    rope = tab is not None
    nqp = N_HEADS * MLA_HEAD_PAD
    const = lambda i, j: (0, 0)
    in_specs = [pl.BlockSpec((1, tm, D_MODEL), lambda i, j: (i, j, 0)),
                pl.BlockSpec((1, 1, 6 * D_MODEL), lambda i, j: (mod_row(i), 0, 0)),
                pl.BlockSpec((1, D_MODEL), const),
                pl.BlockSpec((D_MODEL, Q_LORA), const),
                pl.BlockSpec((1, Q_LORA), const),
                pl.BlockSpec((Q_LORA, nqp), const),
                pl.BlockSpec((D_MODEL, CKR_PAD), const),
                pl.BlockSpec((1, KV_LORA), const)]
    args = [x, mods, g.reshape(1, D_MODEL), wdq.astype(BF16), gq.reshape(1, Q_LORA), wuq_pad,
            wdkv_pad, gkv.reshape(1, KV_LORA)]
    if rope:
        in_specs.append(pl.BlockSpec((4, ROPE_DIM // 4, tm), lambda i, j: (0, 0, j)))
        args.append(tab)
    out_specs = [pl.BlockSpec((1, nqp, tm), lambda i, j: (i, 0, j)),
                 pl.BlockSpec((1, tm, CKR_PAD), lambda i, j: (i, j, 0))]
    out_shape = [jax.ShapeDtypeStruct((b, nqp, t), BF16),
                 jax.ShapeDtypeStruct((b, t, CKR_PAD), BF16)]
    if states:
        out_specs += [pl.BlockSpec((1, tm, KV_LORA), lambda i, j: (i, j, 0)),
                      pl.BlockSpec((1, tm, ROPE_DIM), lambda i, j: (i, j, 0))]
        out_shape += [jax.ShapeDtypeStruct((b, t, KV_LORA), F32),
                      jax.ShapeDtypeStruct((b, t, ROPE_DIM), F32)]
    return pl.pallas_call(
        functools.partial(_proj_mla_kernel, rope=rope, states=states),
        grid=(b, t // tm), in_specs=in_specs, out_specs=out_specs, out_shape=out_shape,
        compiler_params=_params(2), name="proj_mla",
    )(*args)


def _expand_kernel(ck_ref, wk_ref, wv_ref, k_ref, vT_ref):
    ck = ck_ref[0]
    k = _dot(ck, wk_ref[...]).astype(BF16)
    vT = _dot(ck[:, :KV_LORA], wv_ref[...]).T.astype(BF16)
    for p in range(N_HEADS // 2):
        k_ref[0, p] = k[:, p * 2 * MLA_HEAD_PAD:(p + 1) * 2 * MLA_HEAD_PAD]
        vT_ref[0, p, 0] = vT[p * 2 * V_DIM:(p + 1) * 2 * V_DIM, :]


def _expand_mla(ck, wk_exp, wv):
    b, t, _ = ck.shape
    tm = ROW_TILE
    nkp = N_HEADS * MLA_HEAD_PAD
    nv = N_HEADS * V_DIM
    n_pairs = N_HEADS // 2
    return pl.pallas_call(
        _expand_kernel,
        grid=(b, t // tm),
        in_specs=[pl.BlockSpec((1, tm, CKR_PAD), lambda i, j: (i, j, 0)),
                  pl.BlockSpec((CKR_PAD, nkp), lambda i, j: (0, 0)),
                  pl.BlockSpec((KV_LORA, nv), lambda i, j: (0, 0))],
        out_specs=[pl.BlockSpec((1, n_pairs, tm, 2 * MLA_HEAD_PAD), lambda i, j: (i, 0, j, 0)),
                   pl.BlockSpec((1, n_pairs, 1, 2 * V_DIM, tm), lambda i, j: (i, 0, j, 0, 0))],
        out_shape=[jax.ShapeDtypeStruct((b, n_pairs, t, 2 * MLA_HEAD_PAD), BF16),
                   jax.ShapeDtypeStruct((b, n_pairs, t // tm, 2 * V_DIM, tm), BF16)],
        compiler_params=_params(2), name="expand_mla",
    )(ck, wk_exp, wv)


def _q_operand(q_ref, u, lo, width, group):
    qg = jnp.concatenate(
        [q_ref[0, (u * group + g) * HEAD_DIM:(u * group + g + 1) * HEAD_DIM, lo:lo + width] for g in range(group)],
        axis=1)
    z = jnp.zeros_like(qg)
    return jnp.concatenate([qg, z] if u % 2 == 0 else [z, qg], axis=0)


def _attn_dense_kernel(*refs, mla, units, tq, n_blocks, tile_group, has_sink):
    q_ref, k_ref, v_ref = refs[:3]
    pos = 3
    sink_ref = None
    if has_sink:
        sink_ref = refs[pos]
        pos += 1
    o_ref = refs[pos]
    pos += 1
    tile_blocks = min(TILE_BLOCKS, n_blocks)
    nbuf = min(tile_group, pl.cdiv(n_blocks, tile_blocks))
    s_refs = [refs[pos + u * nbuf:pos + (u + 1) * nbuf] for u in range(units)]
    pos += units * nbuf
    p_refs = [refs[pos + u * nbuf:pos + (u + 1) * nbuf] for u in range(units)]
    pos += units * nbuf
    if mla:
        acc_ref, m_ref = refs[pos:]
        group = 1
    else:
        qp_ref, acc_ref, m_ref = refs[pos:]
        group = GROUP
        for u in range(units):
            qp_ref[u] = _q_operand(q_ref, u, 0, tq, group)
    ng = group * tq
    tk = KV_TILE
    cw = SOFTMAX_CHUNK
    sub = SUBLANES
    v_tiles = V_DIM // sub

    for u in range(units):
        m_ref[u] = jnp.broadcast_to(sink_ref[u], (sub, ng)) if has_sink else jnp.full((sub, ng), NEG_INF, F32)
        acc_ref[u, :v_tiles] = jnp.zeros((v_tiles, sub, ng), F32)
        acc_ref[u, v_tiles:] = jnp.full((SUM_ROWS // sub, sub, ng), 1.0 if has_sink else 0.0, F32)

    def scores(b0, nb, par):
        rows = nb * tk
        off = b0 * tk if isinstance(b0, int) else pl.multiple_of(b0 * tk, tk)
        for u in range(units):
            if mla:
                kp = k_ref[0, u // 2, pl.ds(off, rows), (u % 2) * MLA_HEAD_PAD:(u % 2 + 1) * MLA_HEAD_PAD]
                qop = q_ref[0, u * MLA_HEAD_PAD:(u + 1) * MLA_HEAD_PAD, :]
            else:
                kp = k_ref[0, u // 2, pl.ds(off, rows), :]
                qop = qp_ref[u]
            s_refs[u][par][:rows // sub] = _dot(kp, qop).reshape(rows // sub, sub, ng)

    def accumulate(b0, nb, par):
        rows = nb * tk
        for u in range(units):
            s_ref, p_ref = s_refs[u][par], p_refs[u][par]
            m_blk = jnp.max(s_ref[:rows // sub], axis=0)
            for shift in (4, 2, 1):
                m_blk = jnp.maximum(m_blk, pltpu.roll(m_blk, shift, 0))
            m_prev = m_ref[u]
            m_new = jnp.maximum(m_prev, m_blk)
            m_ref[u] = m_new
            alpha = jnp.exp2(m_prev - m_new)
            for c in range(ng // cw):
                cols = slice(c * cw, (c + 1) * cw)
                p = jnp.exp2(s_ref[:rows // sub, :, cols] - m_ref[u, :, cols])
                p_ref[:rows, cols] = p.reshape(rows, cw).astype(BF16)
            v_rows = [v_ref[0, u // 2, b0 + i, (u % 2) * V_DIM:(u % 2 + 1) * V_DIM, :] for i in range(nb)]
            v_ones = jnp.concatenate([jnp.concatenate(v_rows, axis=1), jnp.ones((SUM_ROWS, rows), BF16)], axis=0)
            pv = _dot(v_ones, p_ref[:rows]).reshape(v_tiles + SUM_ROWS // sub, sub, ng)
            acc_ref[u] = alpha * acc_ref[u] + pv

    def tile_group(b0, sizes):
        starts = [b0 + sum(sizes[:r]) for r in range(len(sizes))]
        for r, nb in enumerate(sizes):
            scores(starts[r], nb, r)
        for r, nb in enumerate(sizes):
            accumulate(starts[r], nb, r)

    tiles = [n_blocks % tile_blocks] * bool(n_blocks % tile_blocks) + [tile_blocks] * (n_blocks // tile_blocks)
    head = len(tiles) % nbuf
    if head == 0 and tiles[0] != tile_blocks:
        head = nbuf
    if head:
        tile_group(0, tiles[:head])
    first = sum(tiles[:head])
    n_loop = (len(tiles) - head) // nbuf
    if n_loop == 1:
        tile_group(first, tiles[head:])
    elif n_loop > 1:
        def body(i, carry):
            tile_group(first + nbuf * tile_blocks * i, [tile_blocks] * nbuf)
            return carry
        lax.fori_loop(0, n_loop, body, 0)

    for u in range(units):
        o = (acc_ref[u, :v_tiles] / acc_ref[u, v_tiles]).reshape(V_DIM, ng)
        if mla:
            o_ref[0, u * V_DIM:(u + 1) * V_DIM, :] = o.astype(BF16)
        else:
            for g in range(group):
                r0 = (u * group + g) * HEAD_DIM
                o_ref[0, r0:r0 + HEAD_DIM, :] = o[:, g * tq:(g + 1) * tq].astype(BF16)


def _attn_dense(qT, k, vT, sink_cols, *, mla, tq, pairs=1, pair_major=False, tile_group=TILE_GROUP):
    b, _, t = qT.shape
    tk = KV_TILE
    n = k.shape[2] if pair_major else k.shape[1]
    n_pairs = N_HEADS // 2 if mla else KV_HEADS // 2
    units = 2 * pairs
    q_rows = pairs * (2 * MLA_HEAD_PAD if mla else 2 * GROUP * HEAD_DIM)
    o_rows = pairs * (2 * V_DIM if mla else 2 * GROUP * HEAD_DIM)
    pair_cols = 2 * MLA_HEAD_PAD if mla else 2 * HEAD_DIM
    ng = tq if mla else GROUP * tq
    tile_blocks = min(TILE_BLOCKS, n // tk)
    nbuf = min(tile_group, pl.cdiv(n // tk, tile_blocks))
    tile_rows = tile_blocks * tk
    k_pm = k if pair_major else jnp.swapaxes(k.reshape(b, n, n_pairs, pair_cols), 1, 2)
    v_pm = vT if pair_major else jnp.swapaxes(vT.reshape(b, n // tk, n_pairs, 2 * V_DIM, tk), 1, 2)
    in_specs = [pl.BlockSpec((1, q_rows, tq), lambda i, p, j: (i, p, j)),
                pl.BlockSpec((1, pairs, n, pair_cols), lambda i, p, j: (i, p, 0, 0)),
                pl.BlockSpec((1, pairs, n // tk, 2 * V_DIM, tk), lambda i, p, j: (i, p, 0, 0, 0))]
    args = [qT, k_pm, v_pm]
    if sink_cols is not None:
        in_specs.append(pl.BlockSpec((units, 1, ng), lambda i, p, j: (p, 0, 0)))
        args.append(sink_cols)
    sub = SUBLANES
    scratch = [pltpu.VMEM((tile_rows // sub, sub, ng), F32)] * (units * nbuf)
    scratch += [pltpu.VMEM((tile_rows, ng), BF16)] * (units * nbuf)
    if not mla:
        scratch.append(pltpu.VMEM((units, 2 * HEAD_DIM, ng), BF16))
    scratch += [pltpu.VMEM((units, (V_DIM + SUM_ROWS) // sub, sub, ng), F32),
                pltpu.VMEM((units, sub, ng), F32)]
    return pl.pallas_call(
        functools.partial(_attn_dense_kernel, mla=mla, units=units, tq=tq, n_blocks=n // tk,
                          tile_group=tile_group, has_sink=sink_cols is not None),
        grid=(b, n_pairs // pairs, t // tq), in_specs=in_specs,
        out_specs=pl.BlockSpec((1, o_rows, tq), lambda i, p, j: (i, p, j)),
        out_shape=jax.ShapeDtypeStruct((b, N_HEADS * V_DIM, t), BF16),
        scratch_shapes=scratch, compiler_params=_params(3),
        name="attn_mla" if mla else "attn_gqa",
    )(*args)


def _attn_window_kernel(q_ref, kc_ref, vc_ref, kp_ref, vp_ref, kcur_ref, vcur_ref, kn_ref, vn_ref,
                        sink_ref, o_ref, *scratch, tq, n_steps):
    wb = WINDOW_BLOCK
    step = pl.program_id(2)
    n_sub = tq // wb
    s_refs, p_refs = scratch[:2 * n_sub], scratch[2 * n_sub:]
    ng = GROUP * wb
    k_off = lax.broadcasted_iota(jnp.int32, (wb, ng), 0)
    q_off = lax.broadcasted_iota(jnp.int32, (wb, ng), 1) % wb
    keep_prev = k_off >= q_off
    keep_next = k_off <= q_off
    first = step == 0
    last = step == n_steps - 1
    n_ctx = kc_ref.shape[1]
    ones_rows = jnp.ones((SUM_ROWS, n_ctx + 3 * wb), BF16)

    def kv_block(i):
        if i < 0:
            return kp_ref[0], vp_ref[0, 0]
        if i >= n_sub:
            return kn_ref[0], vn_ref[0, 0]
        c, r = divmod(i * wb, KV_TILE)
        return kcur_ref[0, i * wb:(i + 1) * wb, :], vcur_ref[0, c, :, r:r + wb]

    sub = SUBLANES
    n_keys = n_ctx + 3 * wb
    v_tiles = V_DIM // sub

    def tiles(x):
        return x.reshape(x.shape[0] // sub, sub, x.shape[1])

    dz = jnp.minimum(step, 0)

    for i in range(n_sub):
        kcat = jnp.concatenate([kc_ref[0], kv_block(i - 1)[0], kv_block(i)[0], kv_block(i + 1)[0]], axis=0)
        keep0 = jnp.logical_and(keep_prev, jnp.logical_not(first)) if i == 0 else keep_prev
        keep2 = jnp.logical_and(keep_next, jnp.logical_not(last)) if i == n_sub - 1 else keep_next
        for u in range(2):
            s = _dot(kcat, _q_operand(q_ref, u, i * wb, wb, GROUP))
            s_ref = s_refs[2 * i + u]
            t0, t1, t2 = n_ctx // sub, (n_ctx + wb) // sub, (n_ctx + 2 * wb) // sub
            s_ref[dz, :t0] = tiles(s[:n_ctx])
            s_ref[dz, t0:t1] = tiles(jnp.where(keep0, s[n_ctx:n_ctx + wb], NEG_INF))
            s_ref[dz, t1:t2] = tiles(s[n_ctx + wb:n_ctx + 2 * wb])
            s_ref[dz, t2:] = tiles(jnp.where(keep2, s[n_ctx + 2 * wb:], NEG_INF))

    for i in range(n_sub):
        vcat = jnp.concatenate([vc_ref[0, 0], kv_block(i - 1)[1], kv_block(i)[1], kv_block(i + 1)[1]], axis=1)
        for u in range(2):
            s_ref, p_ref = s_refs[2 * i + u], p_refs[2 * i + u]
            sink = jnp.broadcast_to(sink_ref[u], (sub, ng))
            m = jnp.max(s_ref[dz], axis=0)
            for shift in (4, 2, 1):
                m = jnp.maximum(m, pltpu.roll(m, shift, 0))
            m = jnp.maximum(m, sink)
            p_ref[...] = jnp.exp2(s_ref[dz] - m).reshape(n_keys, ng).astype(BF16)
            v_ones = jnp.concatenate([vcat[u * V_DIM:(u + 1) * V_DIM, :], ones_rows], axis=0)
            r = tiles(_dot(v_ones, p_ref[...]))
            o = (r[:v_tiles] / (r[v_tiles] + jnp.exp2(sink - m))).reshape(V_DIM, ng)
            for g in range(GROUP):
                r0 = (u * GROUP + g) * HEAD_DIM
                o_ref[0, r0:r0 + HEAD_DIM, i * wb:(i + 1) * wb] = o[:, g * wb:(g + 1) * wb].astype(BF16)


def _attn_window(qT, k_ctx, vT_ctx, k, vT, sink_cols, *, tq):
    b, _, t = qT.shape
    wb = WINDOW_BLOCK
    n_ctx = k_ctx.shape[1]
    n_steps = t // tq
    n_sub = tq // wb
    n_wb = t // wb
    per_tile = KV_TILE // wb
    pair_rows = 2 * GROUP * HEAD_DIM
    ng = GROUP * wb

    def prev_blk(j):
        return jnp.maximum(j * n_sub - 1, 0)

    def next_blk(j):
        return jnp.minimum((j + 1) * n_sub, n_wb - 1)

    in_specs = [
        pl.BlockSpec((1, pair_rows, tq), lambda i, p, j: (i, p, j)),
        pl.BlockSpec((1, n_ctx, 2 * HEAD_DIM), lambda i, p, j: (i, 0, p)),
        pl.BlockSpec((1, n_ctx // KV_TILE, 2 * V_DIM, KV_TILE), lambda i, p, j: (i, 0, p, 0)),
        pl.BlockSpec((1, wb, 2 * HEAD_DIM), lambda i, p, j: (i, prev_blk(j), p)),
        pl.BlockSpec((1, 1, 2 * V_DIM, wb), lambda i, p, j: (i, prev_blk(j) // per_tile, p, prev_blk(j) % per_tile)),
        pl.BlockSpec((1, tq, 2 * HEAD_DIM), lambda i, p, j: (i, j, p)),
        pl.BlockSpec((1, tq // KV_TILE, 2 * V_DIM, KV_TILE), lambda i, p, j: (i, j, p, 0)),
        pl.BlockSpec((1, wb, 2 * HEAD_DIM), lambda i, p, j: (i, next_blk(j), p)),
        pl.BlockSpec((1, 1, 2 * V_DIM, wb), lambda i, p, j: (i, next_blk(j) // per_tile, p, next_blk(j) % per_tile)),
        pl.BlockSpec((2, 1, ng), lambda i, p, j: (p, 0, 0)),
    ]
    return pl.pallas_call(
        functools.partial(_attn_window_kernel, tq=tq, n_steps=n_steps),
        grid=(b, KV_HEADS // 2, n_steps), in_specs=in_specs,
        out_specs=pl.BlockSpec((1, pair_rows, tq), lambda i, p, j: (i, p, j)),
        out_shape=jax.ShapeDtypeStruct((b, N_HEADS * HEAD_DIM, t), BF16),
        scratch_shapes=([pltpu.VMEM((1, (n_ctx + 3 * wb) // SUBLANES, SUBLANES, ng), F32)] * (2 * n_sub)
                        + [pltpu.VMEM((n_ctx + 3 * wb, ng), BF16)] * (2 * n_sub)),
        compiler_params=_params(3), name="attn_window",
    )(qT, k_ctx, vT_ctx, k, vT, k, vT, k, vT, sink_cols)


def _post_kernel(*refs, final):
    x_ref, oT_ref, mod_ref, g2_ref, wo_ref, win_ref, wout_ref = refs[:7]
    pos = 7
    if final:
        gf_ref = refs[pos]
        pos += 1
    out_ref = refs[pos]
    mod = mod_ref[0]
    gate, shift2, scale2, gate2 = (mod[:, c * D_MODEL:(c + 1) * D_MODEL] for c in range(2, 6))
    attn = lax.dot_general(oT_ref[0], wo_ref[...], (((0,), (0,)), ((), ())), preferred_element_type=F32)
    x1 = x_ref[0] + gate * attn
    h2 = _norm_mod(x1, g2_ref[...], shift2, scale2).astype(BF16)
    acc = jnp.zeros_like(x1)
    for c in range(D_FF // FF_CHUNK):
        u = jnp.maximum(_dot(h2, win_ref[:, c * FF_CHUNK:(c + 1) * FF_CHUNK]), 0.0)
        acc = acc + _dot((u * u).astype(BF16), wout_ref[c * FF_CHUNK:(c + 1) * FF_CHUNK, :])
    x2 = x1 + gate2 * acc
    out_ref[0] = _rmsnorm(x2, gf_ref[...]) if final else x2


def _post(x, oT, mods, mod_row, g2, wo, win, wout, g_final):
    b, t, _ = x.shape
    tm = min(t, MLP_ROW_TILE)
    final = g_final is not None
    const = lambda i, j: (0, 0)
    in_specs = [pl.BlockSpec((1, tm, D_MODEL), lambda i, j: (i, j, 0)),
                pl.BlockSpec((1, D_MODEL, tm), lambda i, j: (i, 0, j)),
                pl.BlockSpec((1, 1, 6 * D_MODEL), lambda i, j: (mod_row(i), 0, 0)),
                pl.BlockSpec((1, D_MODEL), const),
                pl.BlockSpec((D_MODEL, D_MODEL), const, pipeline_mode=pl.Buffered(1)),
                pl.BlockSpec((D_MODEL, D_FF), const, pipeline_mode=pl.Buffered(1)),
                pl.BlockSpec((D_FF, D_MODEL), const, pipeline_mode=pl.Buffered(1))]
    args = [x, oT, mods, g2.reshape(1, D_MODEL), wo, win, wout]
    if final:
        in_specs.append(pl.BlockSpec((1, D_MODEL), const))
        args.append(g_final.reshape(1, D_MODEL))
    return pl.pallas_call(
        functools.partial(_post_kernel, final=final),
        grid=(b, t // tm), in_specs=in_specs,
        out_specs=pl.BlockSpec((1, tm, D_MODEL), lambda i, j: (i, j, 0)),
        out_shape=jax.ShapeDtypeStruct((b, t, D_MODEL), F32),
        compiler_params=_params(2), name="post_mlp",
    )(*args)


def _sink_cols(sink, width):
    return jnp.repeat((sink.astype(F32) * LOG2_E).reshape(KV_HEADS, 1, GROUP), width, axis=2)


def _mla_weights(w_uq, w_dkv, w_ukv):
    wuq_pad = jnp.pad(w_uq.reshape(Q_LORA, N_HEADS, NOPE_DIM + ROPE_DIM),
                      ((0, 0), (0, 0), (0, MLA_HEAD_PAD - NOPE_DIM - ROPE_DIM)))
    wuq_pad = wuq_pad.reshape(Q_LORA, N_HEADS * MLA_HEAD_PAD).astype(BF16)
    wdkv_pad = jnp.pad(w_dkv, ((0, 0), (0, CKR_PAD - KV_LORA - ROPE_DIM))).astype(BF16)
    w3 = w_ukv.reshape(KV_LORA, N_HEADS, NOPE_DIM + V_DIM)
    wk_nope = jnp.pad(w3[:, :, :NOPE_DIM], ((0, 0), (0, 0), (0, MLA_HEAD_PAD - NOPE_DIM)))
    place = jnp.pad(jnp.eye(ROPE_DIM, dtype=F32), ((0, 0), (NOPE_DIM, MLA_HEAD_PAD - NOPE_DIM - ROPE_DIM)))
    place = jnp.broadcast_to(place[:, None, :], (ROPE_DIM, N_HEADS, MLA_HEAD_PAD))
    wk_exp = jnp.concatenate([wk_nope.reshape(KV_LORA, -1), place.reshape(ROPE_DIM, -1),
                              jnp.zeros((CKR_PAD - KV_LORA - ROPE_DIM, N_HEADS * MLA_HEAD_PAD), F32)], axis=0)
    wv = w3[:, :, NOPE_DIM:].reshape(KV_LORA, N_HEADS * V_DIM)
    return wuq_pad, wdkv_pad, wk_exp.astype(BF16), wv.astype(BF16)


def _ctx_gqa(cache_k, cache_v):
    b, p = cache_k.shape[:2]
    k = cache_k.reshape(b, p, KV_HEADS * HEAD_DIM).astype(BF16)
    v = cache_v.reshape(b, p // KV_TILE, KV_TILE, KV_HEADS * HEAD_DIM)
    return k, jnp.swapaxes(v, 2, 3).astype(BF16)


def kernel(x_prompt, x_sample, c, cache_a_k, cache_a_v, cache_b_ckv, cache_b_krope, cache_c_k, cache_c_v, c_ctx, w_ada, b_ada, norm_g, w_mlp_in, w_mlp_out, a_w_qkv, a_sink, a_w_o, b_w_dq, b_g_q, b_w_uq, b_w_dkv, b_g_kv, b_w_ukv, b_w_o, c_w_qkv, c_g_q, c_g_k, c_w_o, g_final):
    n_batch, seq = x_prompt.shape[:2]
    n_dec, dec_seq = x_sample.shape[:2]
    cond = jnp.concatenate([c_ctx[None, :], c, jnp.zeros((8 - 1 - n_dec, D_MODEL), F32)], axis=0)
    mods_all = _ada_all(cond, w_ada, b_ada)
    tab64 = _rope_tables(dec_seq, HEAD_DIM)
    tab32 = _rope_tables(dec_seq, ROPE_DIM)
    prompt_row = lambda i: 0
    sample_row = lambda i: i + 1

    xp, xs = x_prompt, x_sample
    st = {"a_k": [], "a_v": [], "b_ckv": [], "b_kr": [], "c_k": [], "c_v": []}
    for layer in range(DEPTH):
        kind, j = layer % 3, layer // 3
        mods = mods_all[layer].reshape(8, 1, 6 * D_MODEL)
        g1, g2 = norm_g[layer, 0], norm_g[layer, 1]
        if kind == 0:
            sink = a_sink[j]
            qT, k, vT, ks, vs = _proj_gqa(xp, mods, prompt_row, g1, a_w_qkv[j], None, None, None, states=True)
            st["a_k"].append(ks)
            st["a_v"].append(vs)
            oT_p = _attn_dense(qT, k, vT, _sink_cols(sink, seq), mla=False, tq=seq, pairs=KV_HEADS // 2)
            qT, k, vT = _proj_gqa(xs, mods, sample_row, g1, a_w_qkv[j], tab64, None, None, states=False)
            k_ctx, vT_ctx = _ctx_gqa(cache_a_k[:, j], cache_a_v[:, j])
            oT_s = _attn_window(qT, k_ctx, vT_ctx, k, vT, _sink_cols(sink, WINDOW_BLOCK), tq=1024)
            wo = a_w_o[j]
        elif kind == 1:
            wuq_pad, wdkv_pad, wk_exp, wv = _mla_weights(b_w_uq[j], b_w_dkv[j], b_w_ukv[j])
            qT, ck, cs, krs = _proj_mla(xp, mods, prompt_row, g1, b_w_dq[j], b_g_q[j], wuq_pad, wdkv_pad,
                                        b_g_kv[j], None, states=True)
            st["b_ckv"].append(cs)
            st["b_kr"].append(krs)
            oT_p = _attn_dense(qT, *_expand_mla(ck, wk_exp, wv), None, mla=True, tq=seq, pairs=N_HEADS // 2,
                               pair_major=True)
            qT, ck = _proj_mla(xs, mods, sample_row, g1, b_w_dq[j], b_g_q[j], wuq_pad, wdkv_pad,
                               b_g_kv[j], tab32, states=False)
            ck_ctx = jnp.concatenate(
                [cache_b_ckv[:, j], cache_b_krope[:, j],
                 jnp.zeros(cache_b_krope.shape[:1] + cache_b_krope.shape[2:3] + (CKR_PAD - KV_LORA - ROPE_DIM,), F32)],
                axis=-1).astype(BF16)
            ck_all = jnp.concatenate([ck_ctx, ck], axis=1)
            oT_s = _attn_dense(qT, *_expand_mla(ck_all, wk_exp, wv), None, mla=True, tq=2048, pair_major=True)
            wo = b_w_o[j]
        else:
            qT, k, vT, ks, vs = _proj_gqa(xp, mods, prompt_row, g1, c_w_qkv[j], None, c_g_q[j], c_g_k[j],
                                          states=True)
            st["c_k"].append(ks)
            st["c_v"].append(vs)
            oT_p = _attn_dense(qT, k, vT, None, mla=False, tq=seq, pairs=KV_HEADS // 2)
            qT, k, vT = _proj_gqa(xs, mods, sample_row, g1, c_w_qkv[j], tab64, c_g_q[j], c_g_k[j], states=False)
            k_ctx, vT_ctx = _ctx_gqa(cache_c_k[:, j], cache_c_v[:, j])
            oT_s = _attn_dense(qT, jnp.concatenate([k_ctx, k], axis=1), jnp.concatenate([vT_ctx, vT], axis=1),
                               None, mla=False, tq=1024, tile_group=2)
            wo = c_w_o[j]
        gf = g_final if layer == DEPTH - 1 else None
        wo, win, wout = wo.astype(BF16), w_mlp_in[layer].astype(BF16), w_mlp_out[layer].astype(BF16)
        xp = _post(xp, oT_p, mods, prompt_row, g2, wo, win, wout, gf)
        xs = _post(xs, oT_s, mods, sample_row, g2, wo, win, wout, gf)

    def stack_heads(parts):
        return jnp.stack([p.reshape(n_batch, seq, KV_HEADS, HEAD_DIM) for p in parts], axis=1)

    return (xp, xs, stack_heads(st["a_k"]), stack_heads(st["a_v"]),
            jnp.stack(st["b_ckv"], axis=1), jnp.stack(st["b_kr"], axis=1),
            stack_heads(st["c_k"]), stack_heads(st["c_v"]))
```
